```python
import jax, jax.numpy as jnp
from jax import lax
import numpy as np

D_MODEL = 1024
BATCH = 8
SEQ = 4096
DEPTH = 2

CTX_LEN = 256
GRID_W = 64
N_BRANCH = 3
GLA_WIDTH = D_MODEL
GLA_HEADS = 4
GLA_HV = GLA_WIDTH // GLA_HEADS
GLA_HK = GLA_HV // 2
GLA_RANK = 16
GLA_TAU = 16.0
HGRN_WIDTH = D_MODEL
HGRN_HEADS = D_MODEL // 128
HGRN_HV = HGRN_WIDTH // HGRN_HEADS
HGRN_EXPAND = 128
HGRN_F = HGRN_HEADS * HGRN_EXPAND
ATT_HD = 128
ATT_HQ = D_MODEL // ATT_HD
ATT_HKV = ATT_HQ // 4
ATT_WIDTH = ATT_HQ * ATT_HD
WINDOW = 128
ATT_BLOCK = 128
ROPE_BASE = 10000.0
CHUNK = 16
EPS = 1e-6
IN_SIZES = (GLA_HEADS * GLA_HK, GLA_HEADS * GLA_HK, GLA_WIDTH, GLA_WIDTH, 2 * GLA_RANK,
            HGRN_F, 2 * HGRN_F, HGRN_WIDTH, HGRN_WIDTH,
            ATT_WIDTH, ATT_HKV * ATT_HD, ATT_HKV * ATT_HD, ATT_WIDTH,
            N_BRANCH * D_MODEL)
N_IN = sum(IN_SIZES)

kernel_name = 'hybrid_gla_hgrn2_swa_prefix_dit_block'


def _rms_norm(x, g):
    xf = x.astype(jnp.float32)
    y = xf * lax.rsqrt(jnp.mean(xf * xf, axis=-1, keepdims=True) + EPS)
    return (y * g.astype(jnp.float32)).astype(x.dtype)


def _modulation(cond, w_ada, b_ada):
    m = jnp.matmul(jax.nn.silu(cond), w_ada) + b_ada
    return tuple(t[:, None, :] for t in jnp.split(m, 3, axis=-1))


def _split_in(p):
    return jnp.split(p, np.cumsum(IN_SIZES)[:-1].tolist(), axis=-1)


def _to_heads(t, n_heads):
    bsz, t_len, w = t.shape
    return t.reshape(bsz, t_len, n_heads, w // n_heads).transpose(0, 2, 1, 3)


def _flip(t):
    return jnp.flip(t, axis=2)


def _chunked_recurrence(q, k, v, log_a, s0):
    f32 = jnp.float32
    bsz, nh, t_len, _ = q.shape
    dv = v.shape[-1]
    n = t_len // CHUNK

    def blk(t):
        return t.astype(f32).reshape(bsz, nh, n, CHUNK, t.shape[-1])

    q, k, v, log_a = blk(q), blk(k), blk(v), blk(log_a)
    b = jnp.cumsum(log_a, axis=3)
    b_end = b[:, :, :, -1:, :]
    q_dec = q * jnp.exp(b)
    att = jnp.einsum('bhnik,bhnjk->bhnij', q_dec, k * jnp.exp(-b))
    att = jnp.where(jnp.tril(jnp.ones((CHUNK, CHUNK), dtype=bool)), att, 0.0)
    o_intra = jnp.einsum('bhnij,bhnjv->bhniv', att, v)
    k_end = k * jnp.exp(b_end - b)
    decay = jnp.exp(b_end[:, :, :, 0, :])

    def step(state, inp):
        qd, ke, vc, dc = inp
        o = jnp.einsum('bhik,bhkv->bhiv', qd, state)
        state = dc[..., None] * state + jnp.einsum('bhjk,bhjv->bhkv', ke, vc)
        return state, o

    xs = tuple(jnp.moveaxis(t, 2, 0) for t in (q_dec, k_end, v, decay))
    s_final, o_inter = lax.scan(step, s0, xs)
    o = o_intra + jnp.moveaxis(o_inter, 0, 2)
    return o.reshape(bsz, nh, t_len, dv), s_final


def _bidir_recurrence(lat, ctx):
    qc, kcf, kcb, vc, lcf, lcb = ctx
    q, kf, kb, v, lgf, lgb = lat
    s0 = jnp.zeros(qc.shape[:2] + (qc.shape[-1], vc.shape[-1]), jnp.float32)
    oc_f, s_f = _chunked_recurrence(qc, kcf, vc, lcf, s0)
    oc_b, s_b = _chunked_recurrence(_flip(qc), _flip(kcb), _flip(vc), _flip(lcb), s0)
    o_f, _ = _chunked_recurrence(q, kf, v, lgf, s_f)
    o_b, _ = _chunked_recurrence(_flip(q), _flip(kb), _flip(v), _flip(lgb), s_b)
    return o_f + _flip(o_b), oc_f + _flip(oc_b)


def _gla_streams(q, k, v, lr, w_a2, b_a2):
    q = _to_heads(q, GLA_HEADS) * (GLA_HK ** -0.5)
    k = _to_heads(k, GLA_HEADS)
    v = _to_heads(v, GLA_HEADS)
    bsz, t_len, _ = lr.shape
    lr = lr.reshape(bsz, t_len, 2, GLA_RANK)
    z = jnp.einsum('btdr,drk->dbtk', lr, w_a2) + b_a2[:, None, None, :]
    log_a = jax.nn.log_sigmoid(z.astype(jnp.float32)) / GLA_TAU
    return (q, k, k, v, _to_heads(log_a[0], GLA_HEADS), _to_heads(log_a[1], GLA_HEADS))


def _hgrn_streams(q, f, i, lb):
    bsz, t_len, _ = f.shape
    z = f.reshape(bsz, t_len, 2, HGRN_F).astype(jnp.float32)
    log_f = jnp.logaddexp(jnp.log(lb), jnp.log1p(-lb) + jax.nn.log_sigmoid(z))
    k = -jnp.expm1(log_f)
    h = HGRN_HEADS
    return (_to_heads(q, h), _to_heads(k[:, :, 0], h), _to_heads(k[:, :, 1], h), _to_heads(i, h),
            _to_heads(log_f[:, :, 0], h), _to_heads(log_f[:, :, 1], h))


def _head_norm_gate(o, gain, gate):
    bsz, nh, t_len, dv = o.shape
    o = _rms_norm(jnp.transpose(o, (0, 2, 1, 3)), gain).reshape(bsz, t_len, nh * dv)
    return o.astype(gate.dtype) * jax.nn.silu(gate)


def _rope_1d(u, pos):
    r = u.shape[-1] // 2
    inv = ROPE_BASE ** (-jnp.arange(r, dtype=jnp.float32) / r)
    ang = pos.astype(jnp.float32)[:, None] * inv
    cos, sin = jnp.cos(ang)[:, None, :], jnp.sin(ang)[:, None, :]
    uf = u.astype(jnp.float32)
    u1, u2 = uf[..., :r], uf[..., r:]
    return jnp.concatenate([u1 * cos - u2 * sin, u2 * cos + u1 * sin], axis=-1).astype(u.dtype)


def _axial_rope(t, row, col):
    half = t.shape[-1] // 2
    return jnp.concatenate([_rope_1d(t[..., :half], row), _rope_1d(t[..., half:], col)], axis=-1)


def _window_attention(q, k, v, kc, vc, sink):
    f32 = jnp.float32
    bsz, s_len, _, hd = q.shape
    nb = s_len // ATT_BLOCK
    grp = ATT_HQ // ATT_HKV
    qb = (q * hd ** -0.5).reshape(bsz, nb, ATT_BLOCK, ATT_HKV, grp, hd)

    def band(t):
        tp = jnp.pad(t, ((0, 0), (ATT_BLOCK, ATT_BLOCK), (0, 0), (0, 0)))
        tp = tp.reshape(bsz, nb + 2, ATT_BLOCK, ATT_HKV, hd)
        return jnp.concatenate([tp[:, :-2], tp[:, 1:-1], tp[:, 2:]], axis=2)

    kw, vw = band(k), band(v)
    qi = jnp.arange(ATT_BLOCK)
    kj = jnp.arange(3 * ATT_BLOCK)
    blk = jnp.arange(nb)
    rel = kj[None, :] - ATT_BLOCK - qi[:, None]
    kpos = blk[:, None] * ATT_BLOCK - ATT_BLOCK + kj[None, :]
    mask = (jnp.abs(rel) <= WINDOW)[None] & ((kpos >= 0) & (kpos < s_len))[:, None, :]
    s_loc = jnp.einsum('bnqhgd,bnkhd->bhgnqk', qb, kw).astype(f32)
    s_loc = jnp.where(mask, s_loc, -jnp.inf)
    s_ctx = jnp.einsum('bnqhgd,bchd->bhgnqc', qb, kc).astype(f32)
    sk = jnp.broadcast_to(sink.astype(f32).reshape(1, ATT_HKV, grp, 1, 1, 1), s_ctx.shape[:-1] + (1,))
    p = jax.nn.softmax(jnp.concatenate([sk, s_ctx, s_loc], axis=-1), axis=-1).astype(v.dtype)
    n_ctx = kc.shape[1]
    out = (jnp.einsum('bhgnqc,bchd->bnqhgd', p[..., 1:1 + n_ctx], vc)
           + jnp.einsum('bhgnqk,bnkhd->bnqhgd', p[..., 1 + n_ctx:], vw))
    return out.reshape(bsz, s_len, ATT_HQ * hd)


def _context_attention(qc, kc, vc, sink):
    f32 = jnp.float32
    bsz, n_ctx, _, hd = qc.shape
    grp = ATT_HQ // ATT_HKV
    qs = (qc * hd ** -0.5).reshape(bsz, n_ctx, ATT_HKV, grp, hd)
    s = jnp.einsum('bqhgd,bkhd->bhgqk', qs, kc).astype(f32)
    sk = jnp.broadcast_to(sink.astype(f32).reshape(1, ATT_HKV, grp, 1, 1), s.shape[:-1] + (1,))
    p = jax.nn.softmax(jnp.concatenate([sk, s], axis=-1), axis=-1).astype(vc.dtype)
    out = jnp.einsum('bhgqk,bkhd->bqhgd', p[..., 1:], vc)
    return out.reshape(bsz, n_ctx, ATT_HQ * hd)


def _merge(ys, mg, w_branch, w_out):
    y = jnp.stack(ys, axis=2)
    proj = jnp.einsum('btnw,nwd->btnd', y, w_branch)
    gates = jax.nn.sigmoid(mg.reshape(mg.shape[:-1] + (N_BRANCH, D_MODEL)).astype(jnp.float32))
    merged = jnp.sum(gates.astype(proj.dtype) * proj, axis=2)
    return jnp.matmul(merged, w_out)


def _layer(x, ctx, c, c_ctx, row, col, norm_g, w_ada, b_ada, w_in, gla_w_a2, gla_b_a2, gla_norm_g,
           hgrn_lb, hgrn_norm_g, attn_sink, w_branch, w_out, update_ctx):
    shift, scale, gate = _modulation(c, w_ada, b_ada)
    shift_c, scale_c, gate_c = _modulation(c_ctx[None], w_ada, b_ada)
    h = _rms_norm(x, norm_g) * (1.0 + scale) + shift
    hc = _rms_norm(ctx, norm_g) * (1.0 + scale_c) + shift_c
    (ga_q, ga_k, ga_v, ga_g, ga_lr, hg_q, hg_f, hg_i, hg_g,
     wa_q, wa_k, wa_v, wa_g, mg) = _split_in(jnp.matmul(h, w_in))
    (gac_q, gac_k, gac_v, gac_g, gac_lr, hgc_q, hgc_f, hgc_i, hgc_g,
     wac_q, wac_k, wac_v, wac_g, mgc) = _split_in(jnp.matmul(hc, w_in))

    o_gla, oc_gla = _bidir_recurrence(_gla_streams(ga_q, ga_k, ga_v, ga_lr, gla_w_a2, gla_b_a2),
                                      _gla_streams(gac_q, gac_k, gac_v, gac_lr, gla_w_a2, gla_b_a2))
    y_gla = _head_norm_gate(o_gla, gla_norm_g, ga_g)

    o_hg, oc_hg = _bidir_recurrence(_hgrn_streams(hg_q, hg_f, hg_i, hgrn_lb),
                                    _hgrn_streams(hgc_q, hgc_f, hgc_i, hgrn_lb))
    y_hg = _head_norm_gate(o_hg, hgrn_norm_g, hg_g)

    bsz, s_len, _ = x.shape
    n_ctx = ctx.shape[1]
    q = _axial_rope(wa_q.reshape(bsz, s_len, ATT_HQ, ATT_HD), row, col)
    k = _axial_rope(wa_k.reshape(bsz, s_len, ATT_HKV, ATT_HD), row, col)
    v = wa_v.reshape(bsz, s_len, ATT_HKV, ATT_HD)
    kc = wac_k.reshape(bsz, n_ctx, ATT_HKV, ATT_HD)
    vc = wac_v.reshape(bsz, n_ctx, ATT_HKV, ATT_HD)
    y_att = _window_attention(q, k, v, kc, vc, attn_sink) * jax.nn.silu(wa_g)

    x_new = x + gate * _merge((y_gla, y_hg, y_att), mg, w_branch, w_out)
    if update_ctx:
        yc_gla = _head_norm_gate(oc_gla, gla_norm_g, gac_g)
        yc_hg = _head_norm_gate(oc_hg, hgrn_norm_g, hgc_g)
        qc = wac_q.reshape(bsz, n_ctx, ATT_HQ, ATT_HD)
        yc_att = _context_attention(qc, kc, vc, attn_sink) * jax.nn.silu(wac_g)
        ctx = ctx + gate_c * _merge((yc_gla, yc_hg, yc_att), mgc, w_branch, w_out)
    return x_new, ctx


def setup_inputs(seed: int = 0) -> dict:
    key = jax.random.key(seed)
    ks = jax.random.split(key, 20)
    f32 = jnp.float32

    def nrm(k, shape, scale):
        return jax.random.normal(k, shape, f32) * scale

    d = D_MODEL
    return {
        'x': nrm(ks[0], (BATCH, SEQ, d), 1.0),
        'c': nrm(ks[1], (BATCH, d), 1.0),
        'ctx': nrm(ks[2], (BATCH, CTX_LEN, d), 1.0),
        'c_ctx': nrm(ks[3], (d,), 1.0),
        'norm_g': 1.0 + nrm(ks[4], (DEPTH, d), 0.1),
        'w_ada': nrm(ks[5], (DEPTH, d, 3 * d), 0.5 * d ** -0.5),
        'b_ada': nrm(ks[6], (DEPTH, 3 * d), 0.02),
        'w_in': nrm(ks[7], (DEPTH, d, N_IN), d ** -0.5),
        'gla_w_a2': nrm(ks[8], (DEPTH, 2, GLA_RANK, GLA_HEADS * GLA_HK), GLA_RANK ** -0.5),
        'gla_b_a2': nrm(ks[9], (DEPTH, 2, GLA_HEADS * GLA_HK), 0.1),
        'gla_norm_g': 1.0 + nrm(ks[10], (DEPTH, GLA_HV), 0.1),
        'hgrn_lb_logits': nrm(ks[11], (DEPTH, 2, HGRN_F), 1.0),
        'hgrn_norm_g': 1.0 + nrm(ks[12], (DEPTH, HGRN_HV), 0.1),
        'attn_sink': nrm(ks[13], (DEPTH, ATT_HQ), 1.0),
        'w_branch': nrm(ks[14], (DEPTH, N_BRANCH, D_MODEL, d), D_MODEL ** -0.5),
        'w_out': nrm(ks[15], (DEPTH, d, d), d ** -0.5),
        'final_g': 1.0 + nrm(ks[16], (d,), 0.1),
    }


def reference(x, c, ctx, c_ctx, norm_g, w_ada, b_ada, w_in, gla_w_a2, gla_b_a2, gla_norm_g,
              hgrn_lb_logits, hgrn_norm_g, attn_sink, w_branch, w_out, final_g):
    s_len = x.shape[1]
    n_rows = s_len // GRID_W
    row = jnp.repeat(jnp.arange(n_rows), GRID_W)
    col = jnp.tile(jnp.arange(GRID_W), n_rows)
    lb_cum = jnp.cumsum(jax.nn.softmax(hgrn_lb_logits.astype(jnp.float32), axis=0), axis=0)
    lower_bounds = lb_cum - lb_cum[0]
    for l in range(DEPTH):
        x, ctx = _layer(x, ctx, c, c_ctx, row, col, norm_g[l], w_ada[l], b_ada[l], w_in[l],
                        gla_w_a2[l], gla_b_a2[l], gla_norm_g[l], lower_bounds[l], hgrn_norm_g[l],
                        attn_sink[l], w_branch[l], w_out[l], l < DEPTH - 1)
    return _rms_norm(x, final_g)
```

```python
import functools

import jax
import jax.numpy as jnp
from jax import lax
from jax.experimental import pallas as pl
from jax.experimental.pallas import tpu as pltpu

F32 = jnp.float32
BF16 = jnp.bfloat16

D_MODEL = 1024
BATCH = 8
SEQ = 4096
DEPTH = 2
CTX_LEN = 256
TOK = CTX_LEN + SEQ
GRID_W = 64
N_BRANCH = 3
GLA_HEADS = 4
GLA_HV = 256
GLA_HK = 128
GLA_RANK = 16
GLA_TAU = 16.0
HGRN_HEADS = 8
HGRN_HV = 128
HGRN_HK = 128
HGRN_F = HGRN_HEADS * HGRN_HK
ATT_HD = 128
ATT_HQ = 8
ATT_HKV = 2
ATT_GRP = ATT_HQ // ATT_HKV
WINDOW = 128
ATT_BLOCK = 128
ROPE_BASE = 10000.0
EPS = 1e-6

_IN_SIZES = (512, 512, 1024, 1024, 32, 1024, 2048, 1024, 1024, 1024, 256, 256, 1024, 3072)
_IN_NAMES = ("ga_q", "ga_k", "ga_v", "ga_g", "ga_lr", "hg_q", "hg_f", "hg_i", "hg_g",
             "wa_q", "wa_k", "wa_v", "wa_g", "mg")
_IN_OFF = {}
_o = 0
for _n, _s in zip(_IN_NAMES, _IN_SIZES):
    _IN_OFF[_n] = (_o, _s)
    _o += _s

_P16_ORDER = ("mg", "ga_v", "ga_g", "hg_q", "hg_i", "hg_g", "wa_q", "wa_g", "ga_q", "ga_k", "wa_k", "wa_v")
_P16_OFF = {}
_o = 0
for _n in _P16_ORDER:
    _P16_OFF[_n] = _o
    _o += _IN_OFF[_n][1]
P16_COLS = _o
LR_PAD = 256
P32_COLS = 2 * HGRN_F + LR_PAD

CHUNK = 64
SUB = 16
N_CHUNK = TOK // CHUNK
N_CTX_CHUNK = CTX_LEN // CHUNK
VMEM_LIMIT = 56 * 1024 * 1024


def _nt(a, b):
    return lax.dot_general(a, b, (((1,), (1,)), ((), ())), preferred_element_type=F32)


def _tn(a, b):
    return lax.dot_general(a, b, (((0,), (0,)), ((), ())), preferred_element_type=F32)


def _log_sigmoid(z):
    return jnp.minimum(z, 0.0) - jnp.log1p(jnp.exp(-jnp.abs(z)))


def _silu(x):
    return x * jax.nn.sigmoid(x)


def _mod_kernel(c_ref, w_ref, b_ref, o_ref):
    a = _silu(c_ref[...])
    o_ref[0] = jnp.dot(a.astype(BF16), w_ref[0].astype(BF16), preferred_element_type=F32) + b_ref[0]


def _modulation(cc, w_ada, b_ada):
    tn = 512
    return pl.pallas_call(
        _mod_kernel,
        grid=(DEPTH, 3 * D_MODEL // tn),
        in_specs=[
            pl.BlockSpec((16, D_MODEL), lambda l, j: (0, 0)),
            pl.BlockSpec((1, D_MODEL, tn), lambda l, j: (l, 0, j)),
            pl.BlockSpec((1, 1, tn), lambda l, j: (l, 0, j)),
        ],
        out_specs=pl.BlockSpec((1, 16, tn), lambda l, j: (l, 0, j)),
        out_shape=jax.ShapeDtypeStruct((DEPTH, 16, 3 * D_MODEL), F32),
        name="modulation",
    )(cc, w_ada, b_ada.reshape(DEPTH, 1, 3 * D_MODEL))


ROW_TILE = 256
N_ROW_TILE = TOK // ROW_TILE


def _mod_row(b, j):
    return jnp.where(j < CTX_LEN // ROW_TILE, BATCH, b)


def _norm_kernel(x_ref, g_ref, mod_ref, h_ref):
    x = x_ref[0]
    y = x * lax.rsqrt(jnp.mean(x * x, axis=-1, keepdims=True) + EPS) * g_ref[...]
    m = mod_ref[0]
    shift = m[:, :D_MODEL]
    scale = m[:, D_MODEL:2 * D_MODEL]
    h_ref[0] = (y * (1.0 + scale) + shift).astype(BF16)


def _norm_modulate(xa, norm_g, mod_l):
    return pl.pallas_call(
        _norm_kernel,
        grid=(BATCH, N_ROW_TILE),
        in_specs=[
            pl.BlockSpec((1, ROW_TILE, D_MODEL), lambda b, j: (b, j, 0)),
            pl.BlockSpec((1, D_MODEL), lambda b, j: (0, 0)),
            pl.BlockSpec((1, 1, 3 * D_MODEL), lambda b, j: (_mod_row(b, j), 0, 0)),
        ],
        out_specs=pl.BlockSpec((1, ROW_TILE, D_MODEL), lambda b, j: (b, j, 0)),
        out_shape=jax.ShapeDtypeStruct((BATCH, TOK, D_MODEL), BF16),
        name="norm_modulate",
    )(xa, norm_g.reshape(1, D_MODEL), mod_l)


def _mm_kernel(a_ref, w_ref, o_ref):
    o_ref[...] = jnp.dot(a_ref[...], w_ref[...], preferred_element_type=F32).astype(o_ref.dtype)


def _matmul(a, w, out_dtype, tm, tn):
    m, k = a.shape
    n = w.shape[1]
    return pl.pallas_call(
        _mm_kernel,
        grid=(m // tm, n // tn),
        in_specs=[
            pl.BlockSpec((tm, k), lambda i, j: (i, 0)),
            pl.BlockSpec((k, tn), lambda i, j: (0, j)),
        ],
        out_specs=pl.BlockSpec((tm, tn), lambda i, j: (i, j)),
        out_shape=jax.ShapeDtypeStruct((m, n), out_dtype),
        compiler_params=pltpu.CompilerParams(vmem_limit_bytes=VMEM_LIMIT),
        name="in_proj",
    )(a, w)


def _scan_cumsum(x, reverse):
    n = x.shape[0]
    row = lax.broadcasted_iota(jnp.int32, x.shape, 0)
    d = 1
    while d < n:
        if reverse:
            x = x + jnp.where(row < n - d, pltpu.roll(x, n - d, 0), 0.0)
        else:
            x = x + jnp.where(row >= d, pltpu.roll(x, d, 0), 0.0)
        d *= 2
    return x


def _block_rows(b, s, kind, reverse):
    n, k = b.shape
    nb = n // s
    pieces = []
    for m in range(nb):
        if kind == "before":
            idx = (m + 1) * s if reverse else m * s - 1
            valid = (m < nb - 1) if reverse else (m > 0)
        else:
            idx = m * s if reverse else (m + 1) * s - 1
            valid = True
        if valid:
            pieces.append(jnp.broadcast_to(b[idx:idx + 1, :], (s, k)))
        else:
            pieces.append(jnp.zeros((s, k), b.dtype))
    return pieces[0] if nb == 1 else jnp.concatenate(pieces, axis=0)


def _levels(n):
    out = []
    s = SUB
    while s < n:
        out.append(s)
        s *= 2
    return out


def _chunk_masks(n, reverse):
    i = lax.broadcasted_iota(jnp.int32, (n, n), 0)
    j = lax.broadcasted_iota(jnp.int32, (n, n), 1)
    if reverse:
        i = n - 1 - i
        j = n - 1 - j
    sh = SUB.bit_length() - 1
    masks = {"d": jnp.logical_and(jnp.right_shift(i, sh) == jnp.right_shift(j, sh), j <= i)}
    for s in _levels(n):
        sh = s.bit_length() - 1
        bi = jnp.right_shift(i, sh)
        masks[s] = jnp.logical_and(jnp.bitwise_and(bi, 1) == 1, bi == jnp.right_shift(j, sh) + 1)
    return masks


def _chunk_step(q, k, v, la, st, reverse, masks):
    n = q.shape[0]
    b = _scan_cumsum(la, reverse)
    before = _block_rows(b, SUB, "before", reverse)
    q_sub = (q * jnp.exp(b - before)).astype(BF16)
    k_diag = (k * jnp.exp(before - b)).astype(BF16)
    att = jnp.where(masks["d"], _nt(q_sub, k_diag), 0.0)
    for s in _levels(n):
        if s == SUB:
            q_s = q_sub
        else:
            q_s = (q * jnp.exp(b - _block_rows(b, s, "before", reverse))).astype(BF16)
        k_s = (k * jnp.exp(_block_rows(b, s, "end", reverse) - b)).astype(BF16)
        att = att + jnp.where(masks[s], _nt(q_s, k_s), 0.0)
    total = _block_rows(b, n, "end", reverse)
    q_all = (q * jnp.exp(b)).astype(BF16)
    k_all = (k * jnp.exp(total - b)).astype(BF16)
    o = jnp.dot(att.astype(BF16), v, preferred_element_type=F32) + _nt(q_all, st.astype(BF16))
    st_new = jnp.exp(total[0:1, :]) * st + _tn(v, k_all)
    return o, st_new


def _fwd_chunk(n):
    return n


def _bwd_chunk(n):
    return jnp.where(n < N_CTX_CHUNK, N_CTX_CHUNK - 1 - n, N_CHUNK + N_CTX_CHUNK - 1 - n)


def _gla_kernel(qf_ref, kf_ref, vf_ref, lrf_ref, qb_ref, kb_ref, vb_ref, lrb_ref, wz_ref, bz_ref,
                of_ref, ob_ref, stf_ref, stb_ref):
    @pl.when(pl.program_id(1) == 0)
    def _():
        stf_ref[...] = jnp.zeros_like(stf_ref)
        stb_ref[...] = jnp.zeros_like(stb_ref)

    width = GLA_HEADS * GLA_HK
    for reverse, q_ref, k_ref, v_ref, lr_ref, o_ref, st_ref in (
            (False, qf_ref, kf_ref, vf_ref, lrf_ref, of_ref, stf_ref),
            (True, qb_ref, kb_ref, vb_ref, lrb_ref, ob_ref, stb_ref)):
        masks = _chunk_masks(CHUNK, reverse)
        zo = width if reverse else 0
        z = jnp.dot(lr_ref[0].astype(BF16), wz_ref[:, zo:zo + width],
                    preferred_element_type=F32) + bz_ref[:, zo:zo + width]
        la_all = _log_sigmoid(z) / GLA_TAU
        for h in range(GLA_HEADS):
            ks = slice(h * GLA_HK, (h + 1) * GLA_HK)
            vs = slice(h * GLA_HV, (h + 1) * GLA_HV)
            q = q_ref[0, :, ks].astype(F32) * (GLA_HK ** -0.5)
            k = k_ref[0, :, ks].astype(F32)
            o, st_new = _chunk_step(q, k, v_ref[0, :, vs], la_all[:, ks], st_ref[h], reverse, masks)
            o_ref[0, :, vs] = o
            st_ref[h] = st_new


def _gla(p16, p32, wz, bz):
    c = CHUNK
    q_blk = _P16_OFF["ga_q"] // 512
    k_blk = _P16_OFF["ga_k"] // 512
    v_blk = _P16_OFF["ga_v"] // 1024
    lr_blk = 2 * HGRN_F // LR_PAD

    def spec(width, blk, chunk_of):
        return pl.BlockSpec((1, c, width), lambda b, n: (b, chunk_of(n), blk))

    ins, args = [], []
    for chunk_of in (_fwd_chunk, _bwd_chunk):
        ins += [spec(512, q_blk, chunk_of), spec(512, k_blk, chunk_of), spec(1024, v_blk, chunk_of),
                spec(LR_PAD, lr_blk, chunk_of)]
        args += [p16, p16, p16, p32]
    ins += [pl.BlockSpec((LR_PAD, 1024), lambda b, n: (0, 0)), pl.BlockSpec((1, 1024), lambda b, n: (0, 0))]
    args += [wz, bz]
    out_sds = jax.ShapeDtypeStruct((BATCH, TOK, GLA_HEADS * GLA_HV), F32)
    return pl.pallas_call(
        _gla_kernel,
        grid=(BATCH, N_CHUNK),
        in_specs=ins,
        out_specs=[pl.BlockSpec((1, c, 1024), lambda b, n: (b, _fwd_chunk(n), 0)),
                   pl.BlockSpec((1, c, 1024), lambda b, n: (b, _bwd_chunk(n), 0))],
        out_shape=[out_sds, out_sds],
        scratch_shapes=[pltpu.VMEM((GLA_HEADS, GLA_HV, GLA_HK), F32),
                        pltpu.VMEM((GLA_HEADS, GLA_HV, GLA_HK), F32)],
        compiler_params=pltpu.CompilerParams(dimension_semantics=("arbitrary", "arbitrary"),
                                             vmem_limit_bytes=VMEM_LIMIT),
        name="gla_scan",
    )(*args)


def _hgrn_kernel(qf_ref, if_ref, ff_ref, qb_ref, ib_ref, fb_ref, lb_ref, of_ref, ob_ref, stf_ref, stb_ref):
    @pl.when(pl.program_id(1) == 0)
    def _():
        stf_ref[...] = jnp.zeros_like(stf_ref)
        stb_ref[...] = jnp.zeros_like(stb_ref)

    for d, (reverse, q_ref, i_ref, f_ref, o_ref, st_ref) in enumerate((
            (False, qf_ref, if_ref, ff_ref, of_ref, stf_ref),
            (True, qb_ref, ib_ref, fb_ref, ob_ref, stb_ref))):
        masks = _chunk_masks(CHUNK, reverse)
        log_lb = lb_ref[0, d:d + 1, :]
        log_1m_lb = lb_ref[1, d:d + 1, :]
        for h in range(HGRN_HEADS):
            hs = slice(h * HGRN_HK, (h + 1) * HGRN_HK)
            a = log_lb[:, hs]
            c = log_1m_lb[:, hs] + _log_sigmoid(f_ref[0, :, hs])
            log_f = jnp.maximum(a, c) + jnp.log1p(jnp.exp(-jnp.abs(a - c)))
            k = 1.0 - jnp.exp(log_f)
            q = q_ref[0, :, hs].astype(F32)
            o, st_new = _chunk_step(q, k, i_ref[0, :, hs], log_f, st_ref[h], reverse, masks)
            o_ref[0, :, hs] = o
            st_ref[h] = st_new


def _hgrn(p16, p32, lb_logs):
    c = CHUNK
    q_blk = _P16_OFF["hg_q"] // 1024
    i_blk = _P16_OFF["hg_i"] // 1024

    def spec(blk, chunk_of):
        return pl.BlockSpec((1, c, 1024), lambda b, n: (b, chunk_of(n), blk))

    ins = [spec(q_blk, _fwd_chunk), spec(i_blk, _fwd_chunk), spec(0, _fwd_chunk),
           spec(q_blk, _bwd_chunk), spec(i_blk, _bwd_chunk), spec(1, _bwd_chunk),
           pl.BlockSpec((2, 2, HGRN_F), lambda b, n: (0, 0, 0))]
    out_sds = jax.ShapeDtypeStruct((BATCH, TOK, HGRN_HEADS * HGRN_HV), F32)
    return pl.pallas_call(
        _hgrn_kernel,
        grid=(BATCH, N_CHUNK),
        in_specs=ins,
        out_specs=[pl.BlockSpec((1, c, 1024), lambda b, n: (b, _fwd_chunk(n), 0)),
                   pl.BlockSpec((1, c, 1024), lambda b, n: (b, _bwd_chunk(n), 0))],
        out_shape=[out_sds, out_sds],
        scratch_shapes=[pltpu.VMEM((HGRN_HEADS, HGRN_HV, HGRN_HK), F32),
                        pltpu.VMEM((HGRN_HEADS, HGRN_HV, HGRN_HK), F32)],
        compiler_params=pltpu.CompilerParams(dimension_semantics=("arbitrary", "arbitrary"),
                                             vmem_limit_bytes=VMEM_LIMIT),
        name="hgrn_scan",
    )(p16, p16, p32, p16, p16, p32, lb_logs)


N_ATT_BLK = TOK // ATT_BLOCK
N_CTX_BLK = CTX_LEN // ATT_BLOCK
_FAR = 4 * ATT_BLOCK


def _rope(x, tab):
    lane = lax.broadcasted_iota(jnp.int32, x.shape, 1)
    partner = jnp.where(jnp.bitwise_and(lane, 32) == 0, pltpu.roll(x, 96, 1), pltpu.roll(x, 32, 1))
    return x * tab[:, :ATT_HD] + partner * tab[:, ATT_HD:]


def _attn_kernel(sink_ref, q_ref, kc_ref, vc_ref, kp_ref, k0_ref, kn_ref, vp_ref, v0_ref, vn_ref,
                 g_ref, tq_ref, tp_ref, tn_ref, o_ref):
    kvh = pl.program_id(1)
    j = pl.program_id(2)
    blk = ATT_BLOCK
    tq = tq_ref[...]
    q = q_ref[0].astype(F32)
    qs = jnp.concatenate([_rope(q[:, g * ATT_HD:(g + 1) * ATT_HD], tq) for g in range(ATT_GRP)], axis=0)
    qs = (qs * (ATT_HD ** -0.5)).astype(BF16)
    k_loc = jnp.concatenate([_rope(kp_ref[0].astype(F32), tp_ref[...]),
                             _rope(k0_ref[0].astype(F32), tq),
                             _rope(kn_ref[0].astype(F32), tn_ref[...])], axis=0).astype(BF16)
    k_all = jnp.concatenate([kc_ref[0], k_loc], axis=0)
    v_all = jnp.concatenate([vc_ref[0], vp_ref[0], v0_ref[0], vn_ref[0]], axis=0)
    s = _nt(qs, k_all)

    is_lat = j >= N_CTX_BLK
    off_prev = jnp.where(jnp.logical_and(is_lat, j - 1 >= N_CTX_BLK), 0, _FAR)
    off_cur = jnp.where(is_lat, 0, _FAR)
    off_next = jnp.where(jnp.logical_and(is_lat, j + 1 < N_ATT_BLK), 0, _FAR)
    n_key = CTX_LEN + 3 * blk
    qi = lax.broadcasted_iota(jnp.int32, (blk, n_key), 0)
    col = lax.broadcasted_iota(jnp.int32, (blk, n_key), 1) - CTX_LEN
    land, lor = jnp.logical_and, jnp.logical_or
    in_prev = land(land(col >= 0, col < blk), col >= qi + off_prev)
    in_cur = land(col >= blk + off_cur, col < 2 * blk)
    in_next = land(col >= 2 * blk, col - 2 * blk + off_next <= qi)
    valid = lor(lor(col < 0, in_prev), lor(in_cur, in_next))

    ps, ls = [], []
    for g in range(ATT_GRP):
        sg = jnp.where(valid, s[g * blk:(g + 1) * blk, :], -jnp.inf)
        sk = sink_ref[kvh * ATT_GRP + g]
        m = jnp.maximum(jnp.max(sg, axis=-1, keepdims=True), sk)
        p = jnp.exp(sg - m)
        ls.append(jnp.sum(p, axis=-1, keepdims=True) + jnp.exp(sk - m))
        ps.append(p.astype(BF16))
    out = jnp.dot(jnp.concatenate(ps, axis=0), v_all, preferred_element_type=F32)
    gate = g_ref[0].astype(F32)
    for g in range(ATT_GRP):
        hs = slice(g * ATT_HD, (g + 1) * ATT_HD)
        o_ref[0, :, hs] = (out[g * blk:(g + 1) * blk, :] / ls[g] * _silu(gate[:, hs])).astype(o_ref.dtype)


def _attention(p16, sink, rope_tab):
    blk = ATT_BLOCK
    q_blk = _P16_OFF["wa_q"] // 512
    g_blk = _P16_OFF["wa_g"] // 512
    k_blk = _P16_OFF["wa_k"] // 128
    v_blk = _P16_OFF["wa_v"] // 128

    def prev_of(j):
        return jnp.clip(j - 1, N_CTX_BLK, N_ATT_BLK - 1)

    def next_of(j):
        return jnp.clip(j + 1, N_CTX_BLK, N_ATT_BLK - 1)

    def kv_spec(col0, row_of):
        return pl.BlockSpec((1, blk, ATT_HD), lambda b, h, j: (b, row_of(j), col0 + h))

    def tab_spec(row_of):
        return pl.BlockSpec((blk, 2 * ATT_HD), lambda b, h, j: (row_of(j), 0))

    same = lambda j: j
    ins = [
        pl.BlockSpec(memory_space=pltpu.SMEM),
        pl.BlockSpec((1, blk, ATT_GRP * ATT_HD), lambda b, h, j: (b, j, q_blk + h)),
        pl.BlockSpec((1, CTX_LEN, ATT_HD), lambda b, h, j: (b, 0, k_blk + h)),
        pl.BlockSpec((1, CTX_LEN, ATT_HD), lambda b, h, j: (b, 0, v_blk + h)),
        kv_spec(k_blk, prev_of), kv_spec(k_blk, same), kv_spec(k_blk, next_of),
        kv_spec(v_blk, prev_of), kv_spec(v_blk, same), kv_spec(v_blk, next_of),
        pl.BlockSpec((1, blk, ATT_GRP * ATT_HD), lambda b, h, j: (b, j, g_blk + h)),
        tab_spec(same), tab_spec(prev_of), tab_spec(next_of),
    ]
    return pl.pallas_call(
        _attn_kernel,
        grid=(BATCH, ATT_HKV, N_ATT_BLK),
        in_specs=ins,
        out_specs=pl.BlockSpec((1, blk, ATT_GRP * ATT_HD), lambda b, h, j: (b, j, h)),
        out_shape=jax.ShapeDtypeStruct((BATCH, TOK, ATT_HQ * ATT_HD), BF16),
        compiler_params=pltpu.CompilerParams(vmem_limit_bytes=VMEM_LIMIT),
        name="window_attn",
    )(sink, p16, p16, p16, p16, p16, p16, p16, p16, p16, p16, rope_tab, rope_tab, rope_tab)


def _head_norm(o, gain, heads, width):
    ys = []
    for h in range(heads):
        oh = o[:, h * width:(h + 1) * width]
        ys.append(oh * lax.rsqrt(jnp.mean(oh * oh, axis=-1, keepdims=True) + EPS) * gain)
    return jnp.concatenate(ys, axis=1)


def _merge_kernel(final, ogf_ref, ogb_ref, gg_ref, ohf_ref, ohb_ref, hg_ref, ya_ref, mg_ref, x_ref,
                  mod_ref, gn_gla_ref, gn_hg_ref, wbr_ref, wout_ref, fg_ref, out_ref):
    y_gla = _head_norm(ogf_ref[0] + ogb_ref[0], gn_gla_ref[...], GLA_HEADS, GLA_HV) * _silu(gg_ref[0].astype(F32))
    y_hg = _head_norm(ohf_ref[0] + ohb_ref[0], gn_hg_ref[...], HGRN_HEADS, HGRN_HV) * _silu(hg_ref[0].astype(F32))
    ys = (y_gla.astype(BF16), y_hg.astype(BF16), ya_ref[0])
    merged = None
    for n in range(N_BRANCH):
        proj = jnp.dot(ys[n], wbr_ref[n], preferred_element_type=F32)
        term = jax.nn.sigmoid(mg_ref[0, :, n * D_MODEL:(n + 1) * D_MODEL].astype(F32)) * proj
        merged = term if merged is None else merged + term
    upd = jnp.dot(merged.astype(BF16), wout_ref[...], preferred_element_type=F32)
    x_new = x_ref[0] + mod_ref[0][:, 2 * D_MODEL:] * upd
    if final:
        x_new = x_new * lax.rsqrt(jnp.mean(x_new * x_new, axis=-1, keepdims=True) + EPS) * fg_ref[...]
    out_ref[0] = x_new


def _merge(final, og, oh, y_att, p16, xa, mod_l, gn_gla, gn_hg, wbr, wout, final_g):
    tm = ROW_TILE
    skip = CTX_LEN // tm if final else 0
    n_tiles = N_ROW_TILE - skip

    def rows(width, blk=0):
        return pl.BlockSpec((1, tm, width), lambda b, j: (b, j + skip, blk))

    ins = [
        rows(1024), rows(1024), rows(1024, _P16_OFF["ga_g"] // 1024),
        rows(1024), rows(1024), rows(1024, _P16_OFF["hg_g"] // 1024),
        rows(1024), rows(3072, _P16_OFF["mg"] // 3072), rows(1024),
        pl.BlockSpec((1, 1, 3 * D_MODEL), lambda b, j: (_mod_row(b, j + skip), 0, 0)),
        pl.BlockSpec((1, GLA_HV), lambda b, j: (0, 0)),
        pl.BlockSpec((1, HGRN_HV), lambda b, j: (0, 0)),
        pl.BlockSpec((N_BRANCH, D_MODEL, D_MODEL), lambda b, j: (0, 0, 0)),
        pl.BlockSpec((D_MODEL, D_MODEL), lambda b, j: (0, 0)),
        pl.BlockSpec((1, D_MODEL), lambda b, j: (0, 0)),
    ]
    out_rows = SEQ if final else TOK
    return pl.pallas_call(
        functools.partial(_merge_kernel, final),
        grid=(BATCH, n_tiles),
        in_specs=ins,
        out_specs=pl.BlockSpec((1, tm, D_MODEL), lambda b, j: (b, j, 0)),
        out_shape=jax.ShapeDtypeStruct((BATCH, out_rows, D_MODEL), F32),
        compiler_params=pltpu.CompilerParams(vmem_limit_bytes=VMEM_LIMIT),
        name="merge_final" if final else "merge",
    )(og[0], og[1], p16, oh[0], oh[1], p16, y_att, p16, xa, mod_l,
      gn_gla.reshape(1, GLA_HV), gn_hg.reshape(1, HGRN_HV), wbr, wout, final_g.reshape(1, D_MODEL))


def _rope_table():
    r = ATT_HD // 4
    inv = ROPE_BASE ** (-jnp.arange(r, dtype=F32) / r)
    t = jnp.arange(SEQ)
    ang_row = (t // GRID_W).astype(F32)[:, None] * inv
    ang_col = (t % GRID_W).astype(F32)[:, None] * inv
    cos = jnp.concatenate([jnp.cos(ang_row)] * 2 + [jnp.cos(ang_col)] * 2, axis=1)
    sin = jnp.concatenate([-jnp.sin(ang_row), jnp.sin(ang_row), -jnp.sin(ang_col), jnp.sin(ang_col)], axis=1)
    tab = jnp.concatenate([cos, sin], axis=1)
    ident = jnp.concatenate([jnp.ones((CTX_LEN, ATT_HD), F32), jnp.zeros((CTX_LEN, ATT_HD), F32)], axis=1)
    return jnp.concatenate([ident, tab], axis=0)


def _cols(w, name):
    o, s = _IN_OFF[name]
    return w[:, o:o + s]


def kernel(x, c, ctx, c_ctx, norm_g, w_ada, b_ada, w_in, gla_w_a2, gla_b_a2, gla_norm_g, hgrn_lb_logits,
           hgrn_norm_g, attn_sink, w_branch, w_out, final_g):
    xa = jnp.concatenate([ctx, x], axis=1)
    cc = jnp.zeros((16, D_MODEL), F32).at[:BATCH].set(c).at[BATCH].set(c_ctx)
    mod = _modulation(cc, w_ada, b_ada)
    rope_tab = _rope_table()
    lb_cum = jnp.cumsum(jax.nn.softmax(hgrn_lb_logits.astype(F32), axis=0), axis=0)
    lower_bounds = lb_cum - lb_cum[0]

    out = None
    for l in range(DEPTH):
        final = l == DEPTH - 1
        w = w_in[l]
        w16 = jnp.concatenate([_cols(w, n) for n in _P16_ORDER], axis=1).astype(BF16)
        w32 = jnp.concatenate([_cols(w, "hg_f"), _cols(w, "ga_lr"),
                               jnp.zeros((D_MODEL, LR_PAD - 2 * GLA_RANK), F32)], axis=1).astype(BF16)
        wz = jnp.zeros((LR_PAD, 2 * GLA_HEADS * GLA_HK), F32)
        wz = wz.at[:GLA_RANK, :GLA_HEADS * GLA_HK].set(gla_w_a2[l, 0])
        wz = wz.at[GLA_RANK:2 * GLA_RANK, GLA_HEADS * GLA_HK:].set(gla_w_a2[l, 1]).astype(BF16)
        bz = gla_b_a2[l].reshape(1, 2 * GLA_HEADS * GLA_HK)
        lb = lower_bounds[l]
        lb_logs = jnp.stack([jnp.log(lb), jnp.log1p(-lb)], axis=0)
        mod_l = mod[l].reshape(16, 1, 3 * D_MODEL)

        h = _norm_modulate(xa, norm_g[l], mod_l).reshape(BATCH * TOK, D_MODEL)
        p16 = _matmul(h, w16, BF16, 2048, 512).reshape(BATCH, TOK, P16_COLS)
        p32 = _matmul(h, w32, F32, 2048, 256).reshape(BATCH, TOK, P32_COLS)
        og = _gla(p16, p32, wz, bz)
        oh = _hgrn(p16, p32, lb_logs)
        y_att = _attention(p16, attn_sink[l], rope_tab)
        res = _merge(final, og, oh, y_att, p16, xa, mod_l, gla_norm_g[l], hgrn_norm_g[l],
                     w_branch[l].astype(BF16), w_out[l].astype(BF16), final_g)
        if final:
            out = res
        else:
            xa = res
    return out
```

```python
import functools

import numpy as np
import jax
import jax.numpy as jnp
from jax import lax
from jax.experimental import pallas as pl
from jax.experimental.pallas import tpu as pltpu

F32 = jnp.float32
BF16 = jnp.bfloat16

D_MODEL = 1024
BATCH = 8
SEQ = 4096
DEPTH = 2
CTX_LEN = 256
TOK = CTX_LEN + SEQ
GRID_W = 64
N_BRANCH = 3
GLA_HEADS = 4
GLA_HV = 256
GLA_HK = 128
GLA_RANK = 16
GLA_TAU = 16.0
HGRN_HEADS = 8
HGRN_HV = 128
HGRN_HK = 128
HGRN_F = HGRN_HEADS * HGRN_HK
ATT_HD = 128
ATT_HQ = 8
ATT_HKV = 2
ATT_GRP = ATT_HQ // ATT_HKV
WINDOW = 128
ATT_BLOCK = 128
ROPE_BASE = 10000.0
EPS = 1e-6

_IN_SIZES = (512, 512, 1024, 1024, 32, 1024, 2048, 1024, 1024, 1024, 256, 256, 1024, 3072)
_IN_NAMES = ("ga_q", "ga_k", "ga_v", "ga_g", "ga_lr", "hg_q", "hg_f", "hg_i", "hg_g",
             "wa_q", "wa_k", "wa_v", "wa_g", "mg")
_IN_OFF = {}
_o = 0
for _n, _s in zip(_IN_NAMES, _IN_SIZES):
    _IN_OFF[_n] = (_o, _s)
    _o += _s

_P16_ORDER = ("mg", "ga_v", "ga_g", "hg_q", "hg_i", "hg_g", "wa_q", "wa_g", "ga_q", "ga_k", "wa_k", "wa_v")
_P16_OFF = {}
_o = 0
for _n in _P16_ORDER:
    _P16_OFF[_n] = _o
    _o += _IN_OFF[_n][1]
P16_COLS = _o
LR_PAD = 256
P32_COLS = 2 * HGRN_F + LR_PAD
_Q_SCALE = {"ga_q": GLA_HK ** -0.5, "wa_q": ATT_HD ** -0.5}

CHUNK = 128
SUB = 16
N_SUB = CHUNK // SUB
N_CHUNK = TOK // CHUNK
N_CTX_CHUNK = CTX_LEN // CHUNK
VMEM_LIMIT = 56 * 1024 * 1024


def _nt(a, b):
    return lax.dot_general(a, b, (((1,), (1,)), ((), ())), preferred_element_type=F32)


def _tn(a, b):
    return lax.dot_general(a, b, (((0,), (0,)), ((), ())), preferred_element_type=F32)


def _log_sigmoid(z):
    return jnp.minimum(z, 0.0) - jnp.log(1.0 + jnp.exp(-jnp.abs(z)))


def _silu(x):
    return x * jax.nn.sigmoid(x)


def _mod_kernel(c_ref, w_ref, b_ref, o_ref):
    a = _silu(c_ref[...])
    o_ref[0] = jnp.dot(a.astype(BF16), w_ref[0].astype(BF16), preferred_element_type=F32) + b_ref[0]


def _modulation(cc, w_ada, b_ada):
    tn = 512
    return pl.pallas_call(
        _mod_kernel,
        grid=(DEPTH, 3 * D_MODEL // tn),
        in_specs=[
            pl.BlockSpec((16, D_MODEL), lambda l, j: (0, 0)),
            pl.BlockSpec((1, D_MODEL, tn), lambda l, j: (l, 0, j)),
            pl.BlockSpec((1, 1, tn), lambda l, j: (l, 0, j)),
        ],
        out_specs=pl.BlockSpec((1, 16, tn), lambda l, j: (l, 0, j)),
        out_shape=jax.ShapeDtypeStruct((DEPTH, 16, 3 * D_MODEL), F32),
        name="modulation",
    )(cc, w_ada, b_ada.reshape(DEPTH, 1, 3 * D_MODEL))


ROW_TILE = 256
N_ROW_TILE = TOK // ROW_TILE


def _mod_row(b, j):
    return jnp.where(j < CTX_LEN // ROW_TILE, BATCH, b)


def _norm_kernel(x_ref, g_ref, mod_ref, h_ref):
    x = x_ref[0]
    y = x * lax.rsqrt(jnp.mean(x * x, axis=-1, keepdims=True) + EPS) * g_ref[...]
    m = mod_ref[0]
    shift = m[:, :D_MODEL]
    scale = m[:, D_MODEL:2 * D_MODEL]
    h_ref[0] = (y * (1.0 + scale) + shift).astype(BF16)


def _norm_modulate(xa, norm_g, mod_l):
    return pl.pallas_call(
        _norm_kernel,
        grid=(BATCH, N_ROW_TILE),
        in_specs=[
            pl.BlockSpec((1, ROW_TILE, D_MODEL), lambda b, j: (b, j, 0)),
            pl.BlockSpec((1, D_MODEL), lambda b, j: (0, 0)),
            pl.BlockSpec((1, 1, 3 * D_MODEL), lambda b, j: (_mod_row(b, j), 0, 0)),
        ],
        out_specs=pl.BlockSpec((1, ROW_TILE, D_MODEL), lambda b, j: (b, j, 0)),
        out_shape=jax.ShapeDtypeStruct((BATCH, TOK, D_MODEL), BF16),
        name="norm_modulate",
    )(xa, norm_g.reshape(1, D_MODEL), mod_l)


def _mm_kernel(a_ref, w_ref, o_ref):
    o_ref[...] = jnp.dot(a_ref[...], w_ref[...], preferred_element_type=F32).astype(o_ref.dtype)


def _matmul(a, w, out_dtype, tm, tn):
    m, k = a.shape
    n = w.shape[1]
    return pl.pallas_call(
        _mm_kernel,
        grid=(m // tm, n // tn),
        in_specs=[
            pl.BlockSpec((tm, k), lambda i, j: (i, 0)),
            pl.BlockSpec((k, tn), lambda i, j: (0, j)),
        ],
        out_specs=pl.BlockSpec((tm, tn), lambda i, j: (i, j)),
        out_shape=jax.ShapeDtypeStruct((m, n), out_dtype),
        compiler_params=pltpu.CompilerParams(vmem_limit_bytes=VMEM_LIMIT),
        name="in_proj",
    )(a, w)


_PAIR_LEVELS = tuple(SUB * 2 ** i for i in range(N_SUB.bit_length() - 1))
_GROUPS = tuple(2 ** (i + 1) for i in range(N_SUB.bit_length() - 1))
_FAC_ROWS = -(-8 * (2 * len(_GROUPS) + 1) // 16) * 16


def _level_matrix(reverse):
    nb = N_SUB
    pos = [nb - 1 - m for m in range(nb)] if reverse else list(range(nb))
    mat = np.zeros((_FAC_ROWS, 16), np.float32)
    for gi, r in enumerate(_GROUPS):
        for m in range(nb):
            for m2 in range(nb):
                if m2 // r == m // r:
                    if pos[m2] < pos[m]:
                        mat[8 * gi + m, m2] = 1.0
                    if pos[m2] > pos[m]:
                        mat[8 * (len(_GROUPS) + gi) + m, m2] = 1.0
    mat[8 * 2 * len(_GROUPS), :nb] = 1.0
    return mat


def _local_cumsum(x, reverse):
    n = x.shape[0]
    r8 = jnp.bitwise_and(lax.broadcasted_iota(jnp.int32, x.shape, 0), 7)
    for d in (1, 2, 4):
        if reverse:
            x = x + jnp.where(r8 < 8 - d, pltpu.roll(x, n - d, 0), 0.0)
        else:
            x = x + jnp.where(r8 >= d, pltpu.roll(x, d, 0), 0.0)
    out = []
    for m in range(n // SUB):
        lo = x[SUB * m:SUB * m + 8]
        hi = x[SUB * m + 8:SUB * m + SUB]
        if reverse:
            lo = lo + hi[0:1]
        else:
            hi = hi + lo[7:8]
        out += [lo, hi]
    return jnp.concatenate(out, axis=0)


def _rep_rows(x):
    return jnp.concatenate([jnp.broadcast_to(x[m:m + 1], (SUB, x.shape[1])) for m in range(N_SUB)], axis=0)


def _split3(x):
    hi = x.astype(BF16)
    r1 = x - hi.astype(F32)
    mid = r1.astype(BF16)
    lo = (r1 - mid.astype(F32)).astype(BF16)
    return hi, mid, lo


def _decay_factors(q, k, la, lmat, reverse):
    bl = _local_cumsum(la, reverse)
    last = [SUB * m + (0 if reverse else SUB - 1) for m in range(N_SUB)]
    tot = jnp.concatenate([bl[i:i + 1] for i in last], axis=0)
    tot16 = jnp.concatenate([tot, jnp.zeros((16 - N_SUB, tot.shape[1]), F32)], axis=0)
    logs = sum(jnp.dot(lmat, t, preferred_element_type=F32) for t in _split3(tot16))
    fac = jnp.exp(logs)
    qe = q * jnp.exp(bl)
    ke = k * jnp.exp(_rep_rows(tot) - bl)
    q_lv = {SUB: qe.astype(BF16)}
    k_lv = {SUB: ke.astype(BF16), "diag": (k * jnp.exp(-bl)).astype(BF16)}
    for gi in range(len(_GROUPS)):
        s = 2 * SUB * 2 ** gi
        q_lv[s] = (qe * _rep_rows(fac[8 * gi:8 * gi + N_SUB])).astype(BF16)
        fo = 8 * (len(_GROUPS) + gi)
        k_lv[s] = (ke * _rep_rows(fac[fo:fo + N_SUB])).astype(BF16)
    decay = fac[8 * 2 * len(_GROUPS):8 * 2 * len(_GROUPS) + 1]
    return q_lv, k_lv, decay


def _lead_rows(x, s, reverse):
    first = 0 if reverse else 1
    pieces = [x[(2 * g + first) * s:(2 * g + first + 1) * s] for g in range(CHUNK // (2 * s))]
    return pieces[0] if len(pieces) == 1 else jnp.concatenate(pieces, axis=0)


def _scan_masks(reverse):
    n = CHUNK
    i = lax.broadcasted_iota(jnp.int32, (n, n), 0)
    j = lax.broadcasted_iota(jnp.int32, (n, n), 1)
    sh = SUB.bit_length() - 1
    same = jnp.right_shift(i, sh) == jnp.right_shift(j, sh)
    masks = {"diag": jnp.logical_and(same, j >= i if reverse else j <= i)}
    rc = lax.broadcasted_iota(jnp.int32, (n // 2, n), 0)
    jc = lax.broadcasted_iota(jnp.int32, (n // 2, n), 1)
    for s in _PAIR_LEVELS:
        sh = s.bit_length() - 1
        masks[s] = jnp.right_shift(jc, sh) == 2 * jnp.right_shift(rc, sh) + (1 if reverse else 0)
    return masks


def _intra_scores(q_lv, k_lv, cols, masks, reverse):
    diag = jnp.where(masks["diag"], _nt(q_lv[SUB][:, cols], k_lv["diag"][:, cols]), 0.0)
    lead = {}
    for s in _PAIR_LEVELS:
        lead[s] = jnp.where(masks[s], _nt(_lead_rows(q_lv[s][:, cols], s, reverse), k_lv[s][:, cols]), 0.0)
    blocks = []
    for m in range(N_SUB):
        blk = diag[SUB * m:SUB * (m + 1)]
        for s in _PAIR_LEVELS:
            r = s // SUB
            ms = m // r
            if ms % 2 == (0 if reverse else 1):
                off = (ms // 2) * s + (m % r) * SUB
                blk = blk + lead[s][off:off + SUB]
        blocks.append(blk)
    return jnp.concatenate(blocks, axis=0).astype(BF16)


def _scan_outputs(plans, heads, hk, hv):
    atts = []
    for q_lv, k_lv, decay, v_ref, o_ref, st_ref, masks, reverse in plans:
        atts.append([_intra_scores(q_lv, k_lv, slice(h * hk, (h + 1) * hk), masks, reverse)
                     for h in range(heads)])
    for (q_lv, k_lv, decay, v_ref, o_ref, st_ref, masks, reverse), att in zip(plans, atts):
        for h in range(heads):
            ks = slice(h * hk, (h + 1) * hk)
            vs = slice(h * hv, (h + 1) * hv)
            v = v_ref[0, :, vs]
            st = st_ref[h]
            o_ref[0, :, vs] = (jnp.dot(att[h], v, preferred_element_type=F32)
                               + _nt(q_lv[CHUNK][:, ks], st.astype(BF16))).astype(o_ref.dtype)
            st_ref[h] = decay[:, ks] * st + _tn(v, k_lv[CHUNK][:, ks])


def _fwd_chunk(n):
    return n


def _bwd_chunk(n):
    return jnp.where(n < N_CTX_CHUNK, N_CTX_CHUNK - 1 - n, N_CHUNK + N_CTX_CHUNK - 1 - n)


def _reset_state(stf_ref, stb_ref):
    @pl.when(pl.program_id(1) == 0)
    def _():
        stf_ref[...] = jnp.zeros_like(stf_ref)
        stb_ref[...] = jnp.zeros_like(stb_ref)


def _gla_kernel(qf_ref, kf_ref, vf_ref, lrf_ref, qb_ref, kb_ref, vb_ref, lrb_ref, wz_ref, bz_ref, lmat_ref,
                of_ref, ob_ref, stf_ref, stb_ref):
    _reset_state(stf_ref, stb_ref)
    width = GLA_HEADS * GLA_HK
    plans = []
    for d, (q_ref, k_ref, v_ref, lr_ref, o_ref, st_ref) in enumerate((
            (qf_ref, kf_ref, vf_ref, lrf_ref, of_ref, stf_ref),
            (qb_ref, kb_ref, vb_ref, lrb_ref, ob_ref, stb_ref))):
        reverse = d == 1
        z = jnp.dot(lr_ref[0].astype(BF16), wz_ref[:, d * width:(d + 1) * width],
                    preferred_element_type=F32) + bz_ref[:, d * width:(d + 1) * width]
        la = _log_sigmoid(z) / GLA_TAU
        q_lv, k_lv, decay = _decay_factors(q_ref[0].astype(F32), k_ref[0].astype(F32), la, lmat_ref[d], reverse)
        plans.append((q_lv, k_lv, decay, v_ref, o_ref, st_ref, _scan_masks(reverse), reverse))
    _scan_outputs(plans, GLA_HEADS, GLA_HK, GLA_HV)


def _scan_call(kernel_fn, ins, args, name, heads, hk, hv):
    out_sds = jax.ShapeDtypeStruct((BATCH, TOK, heads * hv), BF16)
    lmat = jnp.asarray(np.stack([_level_matrix(False), _level_matrix(True)]), dtype=BF16)
    return pl.pallas_call(
        kernel_fn,
        grid=(BATCH, N_CHUNK),
        in_specs=ins + [pl.BlockSpec((2, _FAC_ROWS, 16), lambda b, n: (0, 0, 0))],
        out_specs=[pl.BlockSpec((1, CHUNK, heads * hv), lambda b, n: (b, _fwd_chunk(n), 0)),
                   pl.BlockSpec((1, CHUNK, heads * hv), lambda b, n: (b, _bwd_chunk(n), 0))],
        out_shape=[out_sds, out_sds],
        scratch_shapes=[pltpu.VMEM((heads, hv, hk), F32), pltpu.VMEM((heads, hv, hk), F32)],
        compiler_params=pltpu.CompilerParams(dimension_semantics=("arbitrary", "arbitrary"),
                                             vmem_limit_bytes=VMEM_LIMIT),
        name=name,
    )(*args, lmat)


def _gla(p16, p32, wz, bz):
    q_blk = _P16_OFF["ga_q"] // 512
    k_blk = _P16_OFF["ga_k"] // 512
    v_blk = _P16_OFF["ga_v"] // 1024
    lr_blk = 2 * HGRN_F // LR_PAD

    def spec(width, blk, chunk_of):
        return pl.BlockSpec((1, CHUNK, width), lambda b, n: (b, chunk_of(n), blk))

    ins, args = [], []
    for chunk_of in (_fwd_chunk, _bwd_chunk):
        ins += [spec(512, q_blk, chunk_of), spec(512, k_blk, chunk_of), spec(1024, v_blk, chunk_of),
                spec(LR_PAD, lr_blk, chunk_of)]
        args += [p16, p16, p16, p32]
    ins += [pl.BlockSpec((LR_PAD, 1024), lambda b, n: (0, 0)), pl.BlockSpec((1, 1024), lambda b, n: (0, 0))]
    args += [wz, bz]
    return _scan_call(_gla_kernel, ins, args, "gla_scan", GLA_HEADS, GLA_HK, GLA_HV)


def _hgrn_kernel(qf_ref, if_ref, ff_ref, qb_ref, ib_ref, fb_ref, lb_ref, lmat_ref, of_ref, ob_ref,
                 stf_ref, stb_ref):
    _reset_state(stf_ref, stb_ref)
    plans = []
    for d, (q_ref, i_ref, f_ref, o_ref, st_ref) in enumerate((
            (qf_ref, if_ref, ff_ref, of_ref, stf_ref),
            (qb_ref, ib_ref, fb_ref, ob_ref, stb_ref))):
        reverse = d == 1
        a = lb_ref[0, d:d + 1, :]
        c = lb_ref[1, d:d + 1, :] + _log_sigmoid(f_ref[0])
        log_f = jnp.maximum(a, c) + jnp.log(1.0 + jnp.exp(-jnp.abs(a - c)))
        k = 1.0 - jnp.exp(log_f)
        q_lv, k_lv, decay = _decay_factors(q_ref[0].astype(F32), k, log_f, lmat_ref[d], reverse)
        plans.append((q_lv, k_lv, decay, i_ref, o_ref, st_ref, _scan_masks(reverse), reverse))
    _scan_outputs(plans, HGRN_HEADS, HGRN_HK, HGRN_HV)


def _hgrn(p16, p32, lb_logs):
    q_blk = _P16_OFF["hg_q"] // 1024
    i_blk = _P16_OFF["hg_i"] // 1024

    def spec(blk, chunk_of):
        return pl.BlockSpec((1, CHUNK, 1024), lambda b, n: (b, chunk_of(n), blk))

    ins = [spec(q_blk, _fwd_chunk), spec(i_blk, _fwd_chunk), spec(0, _fwd_chunk),
           spec(q_blk, _bwd_chunk), spec(i_blk, _bwd_chunk), spec(1, _bwd_chunk),
           pl.BlockSpec((2, 2, HGRN_F), lambda b, n: (0, 0, 0))]
    args = [p16, p16, p32, p16, p16, p32, lb_logs]
    return _scan_call(_hgrn_kernel, ins, args, "hgrn_scan", HGRN_HEADS, HGRN_HK, HGRN_HV)


N_ATT_BLK = TOK // ATT_BLOCK
N_CTX_BLK = CTX_LEN // ATT_BLOCK
_FAR = 4 * ATT_BLOCK


def _rope(x, tab):
    lane = lax.broadcasted_iota(jnp.int32, x.shape, 1)
    partner = jnp.where(jnp.bitwise_and(lane, 32) == 0, pltpu.roll(x, 96, 1), pltpu.roll(x, 32, 1))
    return x * tab[:, :ATT_HD] + partner * tab[:, ATT_HD:]


def _attn_kernel(sink_ref, q_ref, kc_ref, vc_ref, kp_ref, k0_ref, kn_ref, vp_ref, v0_ref, vn_ref,
                 g_ref, tq_ref, tp_ref, tn_ref, o_ref):
    j = pl.program_id(1)
    blk = ATT_BLOCK
    tq = tq_ref[...]
    tp = tp_ref[...]
    tn = tn_ref[...]

    is_lat = j >= N_CTX_BLK
    off_prev = jnp.where(jnp.logical_and(is_lat, j - 1 >= N_CTX_BLK), 0, _FAR)
    off_cur = jnp.where(is_lat, 0, _FAR)
    off_next = jnp.where(jnp.logical_and(is_lat, j + 1 < N_ATT_BLK), 0, _FAR)
    n_key = CTX_LEN + 3 * blk
    qi = lax.broadcasted_iota(jnp.int32, (blk, n_key), 0)
    col = lax.broadcasted_iota(jnp.int32, (blk, n_key), 1) - CTX_LEN
    land, lor = jnp.logical_and, jnp.logical_or
    in_prev = land(land(col >= 0, col < blk), col >= qi + off_prev)
    in_cur = land(col >= blk + off_cur, col < 2 * blk)
    in_next = land(col >= 2 * blk, col - 2 * blk + off_next <= qi)
    valid = lor(lor(col < 0, in_prev), lor(in_cur, in_next))

    for kvh in range(ATT_HKV):
        kv = slice(kvh * ATT_HD, (kvh + 1) * ATT_HD)
        heads = [kvh * ATT_GRP + g for g in range(ATT_GRP)]
        qs = jnp.concatenate([_rope(q_ref[0, :, h * ATT_HD:(h + 1) * ATT_HD].astype(F32), tq) for h in heads],
                             axis=0).astype(BF16)
        k_loc = jnp.concatenate([_rope(kp_ref[0, :, kv].astype(F32), tp),
                                 _rope(k0_ref[0, :, kv].astype(F32), tq),
                                 _rope(kn_ref[0, :, kv].astype(F32), tn)], axis=0).astype(BF16)
        k_all = jnp.concatenate([kc_ref[0, :, kv], k_loc], axis=0)
        v_all = jnp.concatenate([vc_ref[0, :, kv], vp_ref[0, :, kv], v0_ref[0, :, kv], vn_ref[0, :, kv]], axis=0)
        s = _nt(qs, k_all)
        ps, ls = [], []
        for g, h in enumerate(heads):
            sg = jnp.where(valid, s[g * blk:(g + 1) * blk, :], -jnp.inf)
            sk = sink_ref[h]
            m = jnp.maximum(jnp.max(sg, axis=-1, keepdims=True), sk)
            p = jnp.exp(sg - m)
            ls.append(jnp.sum(p, axis=-1, keepdims=True) + jnp.exp(sk - m))
            ps.append(p.astype(BF16))
        out = jnp.dot(jnp.concatenate(ps, axis=0), v_all, preferred_element_type=F32)
        for g, h in enumerate(heads):
            hs = slice(h * ATT_HD, (h + 1) * ATT_HD)
            gate = g_ref[0, :, hs].astype(F32)
            o_ref[0, :, hs] = (out[g * blk:(g + 1) * blk, :] / ls[g] * _silu(gate)).astype(o_ref.dtype)


def _attention(p16, sink, rope_tab):
    blk = ATT_BLOCK
    qw = ATT_HQ * ATT_HD
    kvw = ATT_HKV * ATT_HD
    q_blk = _P16_OFF["wa_q"] // qw
    g_blk = _P16_OFF["wa_g"] // qw
    k_blk = _P16_OFF["wa_k"] // kvw
    v_blk = _P16_OFF["wa_v"] // kvw

    def prev_of(j):
        return jnp.clip(j - 1, N_CTX_BLK, N_ATT_BLK - 1)

    def next_of(j):
        return jnp.clip(j + 1, N_CTX_BLK, N_ATT_BLK - 1)

    def kv_spec(col, row_of):
        return pl.BlockSpec((1, blk, kvw), lambda b, j: (b, row_of(j), col))

    def tab_spec(row_of):
        return pl.BlockSpec((blk, 2 * ATT_HD), lambda b, j: (row_of(j), 0))

    same = lambda j: j
    ins = [
        pl.BlockSpec(memory_space=pltpu.SMEM),
        pl.BlockSpec((1, blk, qw), lambda b, j: (b, j, q_blk)),
        pl.BlockSpec((1, CTX_LEN, kvw), lambda b, j: (b, 0, k_blk)),
        pl.BlockSpec((1, CTX_LEN, kvw), lambda b, j: (b, 0, v_blk)),
        kv_spec(k_blk, prev_of), kv_spec(k_blk, same), kv_spec(k_blk, next_of),
        kv_spec(v_blk, prev_of), kv_spec(v_blk, same), kv_spec(v_blk, next_of),
        pl.BlockSpec((1, blk, qw), lambda b, j: (b, j, g_blk)),
        tab_spec(same), tab_spec(prev_of), tab_spec(next_of),
    ]
    return pl.pallas_call(
        _attn_kernel,
        grid=(BATCH, N_ATT_BLK),
        in_specs=ins,
        out_specs=pl.BlockSpec((1, blk, qw), lambda b, j: (b, j, 0)),
        out_shape=jax.ShapeDtypeStruct((BATCH, TOK, ATT_HQ * ATT_HD), BF16),
        compiler_params=pltpu.CompilerParams(vmem_limit_bytes=VMEM_LIMIT),
        name="window_attn",
    )(sink, p16, p16, p16, p16, p16, p16, p16, p16, p16, p16, rope_tab, rope_tab, rope_tab)


def _head_norm(o, gain, heads, width):
    ys = []
    for h in range(heads):
        oh = o[:, h * width:(h + 1) * width]
        ys.append(oh * lax.rsqrt(jnp.mean(oh * oh, axis=-1, keepdims=True) + EPS) * gain)
    return jnp.concatenate(ys, axis=1)


def _merge_kernel(final, ogf_ref, ogb_ref, gg_ref, ohf_ref, ohb_ref, hg_ref, ya_ref, mg_ref, x_ref,
                  mod_ref, gn_gla_ref, gn_hg_ref, wbr_ref, wout_ref, fg_ref, out_ref):
    o_gla = ogf_ref[0].astype(F32) + ogb_ref[0].astype(F32)
    o_hg = ohf_ref[0].astype(F32) + ohb_ref[0].astype(F32)
    y_gla = _head_norm(o_gla, gn_gla_ref[...], GLA_HEADS, GLA_HV) * _silu(gg_ref[0].astype(F32))
    y_hg = _head_norm(o_hg, gn_hg_ref[...], HGRN_HEADS, HGRN_HV) * _silu(hg_ref[0].astype(F32))
    ys = (y_gla.astype(BF16), y_hg.astype(BF16), ya_ref[0])
    merged = None
    for n in range(N_BRANCH):
        proj = jnp.dot(ys[n], wbr_ref[n], preferred_element_type=F32)
        term = jax.nn.sigmoid(mg_ref[0, :, n * D_MODEL:(n + 1) * D_MODEL].astype(F32)) * proj
        merged = term if merged is None else merged + term
    upd = jnp.dot(merged.astype(BF16), wout_ref[...], preferred_element_type=F32)
    x_new = x_ref[0] + mod_ref[0][:, 2 * D_MODEL:] * upd
    if final:
        x_new = x_new * lax.rsqrt(jnp.mean(x_new * x_new, axis=-1, keepdims=True) + EPS) * fg_ref[...]
    out_ref[0] = x_new


def _merge(final, og, oh, y_att, p16, xa, mod_l, gn_gla, gn_hg, wbr, wout, final_g):
    tm = ROW_TILE
    skip = CTX_LEN // tm if final else 0
    n_tiles = N_ROW_TILE - skip

    def rows(width, blk=0):
        return pl.BlockSpec((1, tm, width), lambda b, j: (b, j + skip, blk))

    ins = [
        rows(1024), rows(1024), rows(1024, _P16_OFF["ga_g"] // 1024),
        rows(1024), rows(1024), rows(1024, _P16_OFF["hg_g"] // 1024),
        rows(1024), rows(3072, _P16_OFF["mg"] // 3072), rows(1024),
        pl.BlockSpec((1, 1, 3 * D_MODEL), lambda b, j: (_mod_row(b, j + skip), 0, 0)),
        pl.BlockSpec((1, GLA_HV), lambda b, j: (0, 0)),
        pl.BlockSpec((1, HGRN_HV), lambda b, j: (0, 0)),
        pl.BlockSpec((N_BRANCH, D_MODEL, D_MODEL), lambda b, j: (0, 0, 0)),
        pl.BlockSpec((D_MODEL, D_MODEL), lambda b, j: (0, 0)),
        pl.BlockSpec((1, D_MODEL), lambda b, j: (0, 0)),
    ]
    out_rows = SEQ if final else TOK
    return pl.pallas_call(
        functools.partial(_merge_kernel, final),
        grid=(BATCH, n_tiles),
        in_specs=ins,
        out_specs=pl.BlockSpec((1, tm, D_MODEL), lambda b, j: (b, j, 0)),
        out_shape=jax.ShapeDtypeStruct((BATCH, out_rows, D_MODEL), F32),
        compiler_params=pltpu.CompilerParams(vmem_limit_bytes=VMEM_LIMIT),
        name="merge_final" if final else "merge",
    )(og[0], og[1], p16, oh[0], oh[1], p16, y_att, p16, xa, mod_l,
      gn_gla.reshape(1, GLA_HV), gn_hg.reshape(1, HGRN_HV), wbr, wout, final_g.reshape(1, D_MODEL))


def _rope_table():
    r = ATT_HD // 4
    inv = ROPE_BASE ** (-jnp.arange(r, dtype=F32) / r)
    t = jnp.arange(SEQ)
    ang_row = (t // GRID_W).astype(F32)[:, None] * inv
    ang_col = (t % GRID_W).astype(F32)[:, None] * inv
    cos = jnp.concatenate([jnp.cos(ang_row)] * 2 + [jnp.cos(ang_col)] * 2, axis=1)
    sin = jnp.concatenate([-jnp.sin(ang_row), jnp.sin(ang_row), -jnp.sin(ang_col), jnp.sin(ang_col)], axis=1)
    tab = jnp.concatenate([cos, sin], axis=1)
    ident = jnp.concatenate([jnp.ones((CTX_LEN, ATT_HD), F32), jnp.zeros((CTX_LEN, ATT_HD), F32)], axis=1)
    return jnp.concatenate([ident, tab], axis=0)


def _cols(w, name):
    o, s = _IN_OFF[name]
    return w[:, o:o + s] * _Q_SCALE.get(name, 1.0)


def kernel(x, c, ctx, c_ctx, norm_g, w_ada, b_ada, w_in, gla_w_a2, gla_b_a2, gla_norm_g, hgrn_lb_logits,
           hgrn_norm_g, attn_sink, w_branch, w_out, final_g):
    xa = jnp.concatenate([ctx, x], axis=1)
    cc = jnp.zeros((16, D_MODEL), F32).at[:BATCH].set(c).at[BATCH].set(c_ctx)
    mod = _modulation(cc, w_ada, b_ada)
    rope_tab = _rope_table()
    lb_cum = jnp.cumsum(jax.nn.softmax(hgrn_lb_logits.astype(F32), axis=0), axis=0)
    lower_bounds = lb_cum - lb_cum[0]

    out = None
    for l in range(DEPTH):
        final = l == DEPTH - 1
        w = w_in[l]
        w16 = jnp.concatenate([_cols(w, n) for n in _P16_ORDER], axis=1).astype(BF16)
        w32 = jnp.concatenate([_cols(w, "hg_f"), _cols(w, "ga_lr"),
                               jnp.zeros((D_MODEL, LR_PAD - 2 * GLA_RANK), F32)], axis=1).astype(BF16)
        wz = jnp.zeros((LR_PAD, 2 * GLA_HEADS * GLA_HK), F32)
        wz = wz.at[:GLA_RANK, :GLA_HEADS * GLA_HK].set(gla_w_a2[l, 0])
        wz = wz.at[GLA_RANK:2 * GLA_RANK, GLA_HEADS * GLA_HK:].set(gla_w_a2[l, 1]).astype(BF16)
        bz = gla_b_a2[l].reshape(1, 2 * GLA_HEADS * GLA_HK)
        lb = lower_bounds[l]
        lb_logs = jnp.stack([jnp.log(lb), jnp.log1p(-lb)], axis=0)
        mod_l = mod[l].reshape(16, 1, 3 * D_MODEL)

        h = _norm_modulate(xa, norm_g[l], mod_l).reshape(BATCH * TOK, D_MODEL)
        p16 = _matmul(h, w16, BF16, 2048, 512).reshape(BATCH, TOK, P16_COLS)
        p32 = _matmul(h, w32, F32, 2048, 256).reshape(BATCH, TOK, P32_COLS)
        og = _gla(p16, p32, wz, bz)
        oh = _hgrn(p16, p32, lb_logs)
        y_att = _attention(p16, attn_sink[l], rope_tab)
        res = _merge(final, og, oh, y_att, p16, xa, mod_l, gla_norm_g[l], hgrn_norm_g[l],
                     w_branch[l].astype(BF16), w_out[l].astype(BF16), final_g)
        if final:
            out = res
        else:
            xa = res
    return out
```

```python
import functools

import numpy as np
import jax
import jax.numpy as jnp
from jax import lax
from jax.experimental import pallas as pl
from jax.experimental.pallas import tpu as pltpu

F32 = jnp.float32
BF16 = jnp.bfloat16

D_MODEL = 1024
BATCH = 8
SEQ = 4096
DEPTH = 2
CTX_LEN = 256
GRID_W = 64
N_BRANCH = 3
GLA_HEADS = 4
GLA_HV = 256
GLA_HK = 128
GLA_RANK = 16
GLA_TAU = 16.0
HGRN_HEADS = 8
HGRN_HV = 128
HGRN_HK = 128
HGRN_F = HGRN_HEADS * HGRN_HK
ATT_HD = 128
ATT_HQ = 8
ATT_HKV = 2
ATT_GRP = ATT_HQ // ATT_HKV
WINDOW = 128
ATT_BLOCK = 128
ROPE_BASE = 10000.0
EPS = 1e-6

R_LAT = BATCH * SEQ
R_CTX = BATCH * CTX_LEN
ROWS = R_LAT + R_CTX

_IN_SIZES = (512, 512, 1024, 1024, 32, 1024, 2048, 1024, 1024, 1024, 256, 256, 1024, 3072)
_IN_NAMES = ("ga_q", "ga_k", "ga_v", "ga_g", "ga_lr", "hg_q", "hg_f", "hg_i", "hg_g",
             "wa_q", "wa_k", "wa_v", "wa_g", "mg")
_IN_OFF = {}
_o = 0
for _n, _s in zip(_IN_NAMES, _IN_SIZES):
    _IN_OFF[_n] = (_o, _s)
    _o += _s

_P16_ORDER = ("mg", "ga_v", "ga_g", "hg_q", "hg_i", "hg_g", "wa_q", "wa_g", "ga_q", "ga_k", "wa_k", "wa_v")
_P16_OFF = {}
_o = 0
for _n in _P16_ORDER:
    _P16_OFF[_n] = _o
    _o += _IN_OFF[_n][1]
P16_COLS = _o
LR_PAD = 256
P32_COLS = 2 * HGRN_F + LR_PAD
_Q_SCALE = {"ga_q": GLA_HK ** -0.5, "wa_q": ATT_HD ** -0.5}

VMEM_LIMIT = 56 * 1024 * 1024


def _nt(a, b):
    return lax.dot_general(a, b, (((1,), (1,)), ((), ())), preferred_element_type=F32)


def _tn(a, b):
    return lax.dot_general(a, b, (((0,), (0,)), ((), ())), preferred_element_type=F32)


def _log_sigmoid(z):
    return jnp.minimum(z, 0.0) - jnp.log(1.0 + jnp.exp(-jnp.abs(z)))


def _silu(x):
    return x * jax.nn.sigmoid(x)


def _norm_mod(x, g, m):
    y = x * lax.rsqrt(jnp.mean(x * x, axis=-1, keepdims=True) + EPS) * g
    return (y * (1.0 + m[:, D_MODEL:2 * D_MODEL]) + m[:, :D_MODEL]).astype(BF16)


def _mod_kernel(c_ref, w_ref, b_ref, o_ref):
    a = _silu(c_ref[...])
    o_ref[0] = jnp.dot(a.astype(BF16), w_ref[0].astype(BF16), preferred_element_type=F32) + b_ref[0]


def _modulation(cc, w_ada, b_ada):
    tn = 512
    return pl.pallas_call(
        _mod_kernel,
        grid=(DEPTH, 3 * D_MODEL // tn),
        in_specs=[
            pl.BlockSpec((16, D_MODEL), lambda l, j: (0, 0)),
            pl.BlockSpec((1, D_MODEL, tn), lambda l, j: (l, 0, j)),
            pl.BlockSpec((1, 1, tn), lambda l, j: (l, 0, j)),
        ],
        out_specs=pl.BlockSpec((1, 16, tn), lambda l, j: (l, 0, j)),
        out_shape=jax.ShapeDtypeStruct((DEPTH, 16, 3 * D_MODEL), F32),
        name="modulation",
    )(cc, w_ada, b_ada.reshape(DEPTH, 1, 3 * D_MODEL))


LAT_TILE = 512
CTX_TILE = 256


def _lat_mod_row(t):
    return t // (SEQ // LAT_TILE)


def _ctx_mod_row(t):
    return BATCH


def _norm_kernel(x_ref, g_ref, mod_ref, h_ref):
    h_ref[...] = _norm_mod(x_ref[...], g_ref[...], mod_ref[0])


def _norm_modulate(x2d, norm_g, mod_l, tm, mod_row):
    rows = x2d.shape[0]
    return pl.pallas_call(
        _norm_kernel,
        grid=(rows // tm,),
        in_specs=[
            pl.BlockSpec((tm, D_MODEL), lambda t: (t, 0)),
            pl.BlockSpec((1, D_MODEL), lambda t: (0, 0)),
            pl.BlockSpec((1, 1, 3 * D_MODEL), lambda t: (mod_row(t), 0, 0)),
        ],
        out_specs=pl.BlockSpec((tm, D_MODEL), lambda t: (t, 0)),
        out_shape=jax.ShapeDtypeStruct((rows, D_MODEL), BF16),
        name="norm_modulate",
    )(x2d, norm_g.reshape(1, D_MODEL), mod_l)


PROJ_TILE = 2048
N_LAT_PROJ = R_LAT // PROJ_TILE
N_CTX_PROJ = R_CTX // PROJ_TILE


def _mm_kernel(hl_ref, hc_ref, w_ref, o_ref):
    i = pl.program_id(0)

    @pl.when(i < N_LAT_PROJ)
    def _():
        o_ref[...] = jnp.dot(hl_ref[...], w_ref[...], preferred_element_type=F32).astype(o_ref.dtype)

    @pl.when(i >= N_LAT_PROJ)
    def _():
        o_ref[...] = jnp.dot(hc_ref[...], w_ref[...], preferred_element_type=F32).astype(o_ref.dtype)


def _in_proj(hl, hc, w, out_dtype, tn):
    k, n = w.shape
    tm = PROJ_TILE
    return pl.pallas_call(
        _mm_kernel,
        grid=(N_LAT_PROJ + N_CTX_PROJ, n // tn),
        in_specs=[
            pl.BlockSpec((tm, k), lambda i, j: (jnp.minimum(i, N_LAT_PROJ - 1), 0)),
            pl.BlockSpec((tm, k), lambda i, j: (jnp.maximum(i - N_LAT_PROJ, 0), 0)),
            pl.BlockSpec((k, tn), lambda i, j: (0, j)),
        ],
        out_specs=pl.BlockSpec((tm, tn), lambda i, j: (i, j)),
        out_shape=jax.ShapeDtypeStruct((ROWS, n), out_dtype),
        compiler_params=pltpu.CompilerParams(vmem_limit_bytes=VMEM_LIMIT),
        name="in_proj",
    )(hl, hc, w)


CHUNK = 128
SUB = 16
N_SUB = CHUNK // SUB
STEP_CHUNKS = 2
STEP_ROWS = STEP_CHUNKS * CHUNK
LAT_CHUNKS = SEQ // STEP_ROWS
CTX_CHUNKS = CTX_LEN // STEP_ROWS
N_CHUNK = LAT_CHUNKS + CTX_CHUNKS
CTX_CHUNK0 = R_LAT // STEP_ROWS

_PAIR_LEVELS = tuple(SUB * 2 ** i for i in range(N_SUB.bit_length() - 1))
_GROUPS = tuple(2 ** (i + 1) for i in range(N_SUB.bit_length() - 1))
_FAC_ROWS = -(-8 * (2 * len(_GROUPS) + 1) // 16) * 16


def _fwd_rows(b, n):
    return jnp.where(n < CTX_CHUNKS, CTX_CHUNK0 + CTX_CHUNKS * b + n, LAT_CHUNKS * b + n - CTX_CHUNKS)


def _bwd_rows(b, n):
    return jnp.where(n < CTX_CHUNKS, CTX_CHUNK0 + CTX_CHUNKS * b + CTX_CHUNKS - 1 - n,
                     LAT_CHUNKS * b + N_CHUNK - 1 - n)


def _level_matrix(reverse):
    nb = N_SUB
    pos = [nb - 1 - m for m in range(nb)] if reverse else list(range(nb))
    mat = np.zeros((_FAC_ROWS, 16), np.float32)
    for gi, r in enumerate(_GROUPS):
        for m in range(nb):
            for m2 in range(nb):
                if m2 // r == m // r:
                    if pos[m2] < pos[m]:
                        mat[8 * gi + m, m2] = 1.0
                    if pos[m2] > pos[m]:
                        mat[8 * (len(_GROUPS) + gi) + m, m2] = 1.0
    mat[8 * 2 * len(_GROUPS), :nb] = 1.0
    return mat


def _cumsum_matrix(reverse):
    i = np.arange(CHUNK)[:, None]
    j = np.arange(CHUNK)[None, :]
    t = ((i // SUB == j // SUB) & ((j >= i) if reverse else (j <= i))).astype(np.float32)
    return np.concatenate([t, t], axis=1)


def _local_cumsum(x, reverse):
    n = x.shape[0]
    r8 = jnp.bitwise_and(lax.broadcasted_iota(jnp.int32, x.shape, 0), 7)
    for d in (1, 2, 4):
        if reverse:
            x = x + jnp.where(r8 < 8 - d, pltpu.roll(x, n - d, 0), 0.0)
        else:
            x = x + jnp.where(r8 >= d, pltpu.roll(x, d, 0), 0.0)
    out = []
    for m in range(n // SUB):
        lo = x[SUB * m:SUB * m + 8]
        hi = x[SUB * m + 8:SUB * m + SUB]
        if reverse:
            lo = lo + hi[0:1]
        else:
            hi = hi + lo[7:8]
        out += [lo, hi]
    return jnp.concatenate(out, axis=0)


def _rep_rows(x):
    return jnp.concatenate([jnp.broadcast_to(x[m:m + 1], (SUB, x.shape[1])) for m in range(N_SUB)], axis=0)


def _split3(x):
    hi = x.astype(BF16)
    r1 = x - hi.astype(F32)
    mid = r1.astype(BF16)
    lo = (r1 - mid.astype(F32)).astype(BF16)
    return hi, mid, lo


def _decay_factors(q, k, la, cmat, lmat, reverse):
    if cmat is None:
        bl = _local_cumsum(la, reverse)
    else:
        hi = la.astype(BF16)
        lo = (la - hi.astype(F32)).astype(BF16)
        bl = jnp.dot(cmat, jnp.concatenate([hi, lo], axis=0), preferred_element_type=F32)
    last = [SUB * m + (0 if reverse else SUB - 1) for m in range(N_SUB)]
    tot = jnp.concatenate([bl[i:i + 1] for i in last], axis=0)
    tot16 = jnp.concatenate([tot, jnp.zeros((16 - N_SUB, tot.shape[1]), F32)], axis=0)
    logs = sum(jnp.dot(lmat, t, preferred_element_type=F32) for t in _split3(tot16))
    fac = jnp.exp(logs)
    qe = q * jnp.exp(bl)
    ke = k * jnp.exp(_rep_rows(tot) - bl)
    q_lv = {SUB: qe.astype(BF16)}
    k_lv = {SUB: ke.astype(BF16), "diag": (k * jnp.exp(-bl)).astype(BF16)}
    for gi in range(len(_GROUPS)):
        s = 2 * SUB * 2 ** gi
        q_lv[s] = (qe * _rep_rows(fac[8 * gi:8 * gi + N_SUB])).astype(BF16)
        fo = 8 * (len(_GROUPS) + gi)
        k_lv[s] = (ke * _rep_rows(fac[fo:fo + N_SUB])).astype(BF16)
    decay = fac[8 * 2 * len(_GROUPS):8 * 2 * len(_GROUPS) + 1]
    return q_lv, k_lv, decay


def _lead_rows(x, s, reverse):
    first = 0 if reverse else 1
    pieces = [x[(2 * g + first) * s:(2 * g + first + 1) * s] for g in range(CHUNK // (2 * s))]
    return pieces[0] if len(pieces) == 1 else jnp.concatenate(pieces, axis=0)


def _scan_masks(reverse):
    n = CHUNK
    i = lax.broadcasted_iota(jnp.int32, (n, n), 0)
    j = lax.broadcasted_iota(jnp.int32, (n, n), 1)
    sh = SUB.bit_length() - 1
    same = jnp.right_shift(i, sh) == jnp.right_shift(j, sh)
    masks = {"diag": jnp.logical_and(same, j >= i if reverse else j <= i)}
    rc = lax.broadcasted_iota(jnp.int32, (n // 2, n), 0)
    jc = lax.broadcasted_iota(jnp.int32, (n // 2, n), 1)
    for s in _PAIR_LEVELS:
        sh = s.bit_length() - 1
        masks[s] = jnp.right_shift(jc, sh) == 2 * jnp.right_shift(rc, sh) + (1 if reverse else 0)
    return masks


def _intra_scores(q_lv, k_lv, cols, masks, reverse):
    diag = jnp.where(masks["diag"], _nt(q_lv[SUB][:, cols], k_lv["diag"][:, cols]), 0.0)
    lead = {}
    for s in _PAIR_LEVELS:
        lead[s] = jnp.where(masks[s], _nt(_lead_rows(q_lv[s][:, cols], s, reverse), k_lv[s][:, cols]), 0.0)
    blocks = []
    for m in range(N_SUB):
        blk = diag[SUB * m:SUB * (m + 1)]
        for s in _PAIR_LEVELS:
            r = s // SUB
            ms = m // r
            if ms % 2 == (0 if reverse else 1):
                off = (ms // 2) * s + (m % r) * SUB
                blk = blk + lead[s][off:off + SUB]
        blocks.append(blk)
    return jnp.concatenate(blocks, axis=0).astype(BF16)


def _step_chunks(reverse):
    order = range(STEP_CHUNKS - 1, -1, -1) if reverse else range(STEP_CHUNKS)
    return [slice(c * CHUNK, (c + 1) * CHUNK) for c in order]


def _scan_outputs(plans, heads, hk, hv):
    atts = []
    for q_lv, k_lv, decay, v_ref, o_ref, rows, st_ref, masks, reverse in plans:
        atts.append([_intra_scores(q_lv, k_lv, slice(h * hk, (h + 1) * hk), masks, reverse)
                     for h in range(heads)])
    for (q_lv, k_lv, decay, v_ref, o_ref, rows, st_ref, masks, reverse), att in zip(plans, atts):
        for h in range(heads):
            ks = slice(h * hk, (h + 1) * hk)
            vs = slice(h * hv, (h + 1) * hv)
            v = v_ref[rows, vs]
            st = st_ref[h]
            o_ref[rows, vs] = (jnp.dot(att[h], v, preferred_element_type=F32)
                               + _nt(q_lv[CHUNK][:, ks], st.astype(BF16))).astype(o_ref.dtype)
            st_ref[h] = decay[:, ks] * st + _tn(v, k_lv[CHUNK][:, ks])


def _reset_state(stf_ref, stb_ref):
    @pl.when(pl.program_id(1) == 0)
    def _():
        stf_ref[...] = jnp.zeros_like(stf_ref)
        stb_ref[...] = jnp.zeros_like(stb_ref)


def _gla_kernel(qf_ref, kf_ref, vf_ref, lrf_ref, qb_ref, kb_ref, vb_ref, lrb_ref, wz_ref, bz_ref,
                lmat_ref, of_ref, ob_ref, stf_ref, stb_ref):
    _reset_state(stf_ref, stb_ref)
    width = GLA_HEADS * GLA_HK
    plans = []
    for d, (q_ref, k_ref, v_ref, lr_ref, o_ref, st_ref) in enumerate((
            (qf_ref, kf_ref, vf_ref, lrf_ref, of_ref, stf_ref),
            (qb_ref, kb_ref, vb_ref, lrb_ref, ob_ref, stb_ref))):
        reverse = d == 1
        masks = _scan_masks(reverse)
        z = jnp.dot(lr_ref[...].astype(BF16), wz_ref[:, d * width:(d + 1) * width],
                    preferred_element_type=F32) + bz_ref[:, d * width:(d + 1) * width]
        la = _log_sigmoid(z) / GLA_TAU
        for rows in _step_chunks(reverse):
            q_lv, k_lv, decay = _decay_factors(q_ref[rows, :].astype(F32), k_ref[rows, :].astype(F32), la[rows],
                                               None, lmat_ref[d], reverse)
            plans.append((q_lv, k_lv, decay, v_ref, o_ref, rows, st_ref, masks, reverse))
    _scan_outputs(plans, GLA_HEADS, GLA_HK, GLA_HV)


def _scan_call(kernel_fn, ins, args, name, heads, hk, hv, mxu_cumsum):
    out_sds = jax.ShapeDtypeStruct((ROWS, heads * hv), BF16)
    consts = []
    if mxu_cumsum:
        consts.append(np.stack([_cumsum_matrix(False), _cumsum_matrix(True)]))
    consts.append(np.stack([_level_matrix(False), _level_matrix(True)]))
    return pl.pallas_call(
        kernel_fn,
        grid=(BATCH, N_CHUNK),
        in_specs=ins + [pl.BlockSpec(c.shape, lambda b, n: (0, 0, 0)) for c in consts],
        out_specs=[pl.BlockSpec((STEP_ROWS, heads * hv), lambda b, n: (_fwd_rows(b, n), 0)),
                   pl.BlockSpec((STEP_ROWS, heads * hv), lambda b, n: (_bwd_rows(b, n), 0))],
        out_shape=[out_sds, out_sds],
        scratch_shapes=[pltpu.VMEM((heads, hv, hk), F32), pltpu.VMEM((heads, hv, hk), F32)],
        compiler_params=pltpu.CompilerParams(dimension_semantics=("arbitrary", "arbitrary"),
                                             vmem_limit_bytes=VMEM_LIMIT),
        name=name,
    )(*args, *[jnp.asarray(c, dtype=BF16) for c in consts])


def _chunk_spec(width, blk, rows_of):
    return pl.BlockSpec((STEP_ROWS, width), lambda b, n: (rows_of(b, n), blk))


def _gla(p16, p32, wz, bz):
    q_blk = _P16_OFF["ga_q"] // 512
    k_blk = _P16_OFF["ga_k"] // 512
    v_blk = _P16_OFF["ga_v"] // 1024
    lr_blk = 2 * HGRN_F // LR_PAD
    ins, args = [], []
    for rows_of in (_fwd_rows, _bwd_rows):
        ins += [_chunk_spec(512, q_blk, rows_of), _chunk_spec(512, k_blk, rows_of),
                _chunk_spec(1024, v_blk, rows_of), _chunk_spec(LR_PAD, lr_blk, rows_of)]
        args += [p16, p16, p16, p32]
    ins += [pl.BlockSpec((LR_PAD, 1024), lambda b, n: (0, 0)), pl.BlockSpec((1, 1024), lambda b, n: (0, 0))]
    args += [wz, bz]
    return _scan_call(_gla_kernel, ins, args, "gla_scan", GLA_HEADS, GLA_HK, GLA_HV, False)


def _hgrn_kernel(qf_ref, if_ref, ff_ref, qb_ref, ib_ref, fb_ref, lb_ref, cmat_ref, lmat_ref, of_ref, ob_ref,
                 stf_ref, stb_ref):
    _reset_state(stf_ref, stb_ref)
    plans = []
    for d, (q_ref, i_ref, f_ref, o_ref, st_ref) in enumerate((
            (qf_ref, if_ref, ff_ref, of_ref, stf_ref),
            (qb_ref, ib_ref, fb_ref, ob_ref, stb_ref))):
        reverse = d == 1
        masks = _scan_masks(reverse)
        a = lb_ref[0, d:d + 1, :]
        for rows in _step_chunks(reverse):
            c = lb_ref[1, d:d + 1, :] + _log_sigmoid(f_ref[rows, :])
            log_f = jnp.maximum(a, c) + jnp.log(1.0 + jnp.exp(-jnp.abs(a - c)))
            k = 1.0 - jnp.exp(log_f)
            q_lv, k_lv, decay = _decay_factors(q_ref[rows, :].astype(F32), k, log_f, cmat_ref[d], lmat_ref[d],
                                               reverse)
            plans.append((q_lv, k_lv, decay, i_ref, o_ref, rows, st_ref, masks, reverse))
    _scan_outputs(plans, HGRN_HEADS, HGRN_HK, HGRN_HV)


def _hgrn(p16, p32, lb_logs):
    q_blk = _P16_OFF["hg_q"] // 1024
    i_blk = _P16_OFF["hg_i"] // 1024
    ins = [_chunk_spec(1024, q_blk, _fwd_rows), _chunk_spec(1024, i_blk, _fwd_rows), _chunk_spec(1024, 0, _fwd_rows),
           _chunk_spec(1024, q_blk, _bwd_rows), _chunk_spec(1024, i_blk, _bwd_rows), _chunk_spec(1024, 1, _bwd_rows),
           pl.BlockSpec((2, 2, HGRN_F), lambda b, n: (0, 0, 0))]
    args = [p16, p16, p32, p16, p16, p32, lb_logs]
    return _scan_call(_hgrn_kernel, ins, args, "hgrn_scan", HGRN_HEADS, HGRN_HK, HGRN_HV, True)


Q_STEP = 2 * ATT_BLOCK
LAT_BLKS = SEQ // ATT_BLOCK
LAT_STEPS = SEQ // Q_STEP
CTX_STEPS = CTX_LEN // Q_STEP
_FAR = 4 * ATT_BLOCK


def _rope(x, tab):
    lane = lax.broadcasted_iota(jnp.int32, x.shape, 1)
    partner = jnp.where(jnp.bitwise_and(lane, 32) == 0, pltpu.roll(x, 96, 1), pltpu.roll(x, 32, 1))
    return x * tab[:, :ATT_HD] + partner * tab[:, ATT_HD:]


def _window_mask(jblk, is_lat):
    blk = ATT_BLOCK
    off_prev = jnp.where(jnp.logical_and(is_lat, jblk >= 1), 0, _FAR)
    off_cur = jnp.where(is_lat, 0, _FAR)
    off_next = jnp.where(jnp.logical_and(is_lat, jblk + 1 < LAT_BLKS), 0, _FAR)
    n_key = CTX_LEN + 3 * blk
    qi = lax.broadcasted_iota(jnp.int32, (blk, n_key), 0)
    col = lax.broadcasted_iota(jnp.int32, (blk, n_key), 1) - CTX_LEN
    land, lor = jnp.logical_and, jnp.logical_or
    in_prev = land(land(col >= 0, col < blk), col >= qi + off_prev)
    in_cur = land(col >= blk + off_cur, col < 2 * blk)
    in_next = land(col >= 2 * blk, col - 2 * blk + off_next <= qi)
    return lor(lor(col < 0, in_prev), lor(in_cur, in_next))


def _attn_kernel(sink_ref, q_ref, g_ref, kvc_ref, kvm_ref, kvp_ref, kvn_ref, tm_ref, tp_ref, tn_ref, o_ref):
    step = pl.program_id(1)
    is_lat = step < LAT_STEPS
    blk = ATT_BLOCK
    kvw = ATT_HKV * ATT_HD
    tmid = tm_ref[...]
    tabs = [tp_ref[...], tmid[:blk], tmid[blk:], tn_ref[...]]
    masks = [_window_mask(2 * step + sub, is_lat) for sub in range(2)]
    for kvh in range(ATT_HKV):
        kc = slice(kvh * ATT_HD, (kvh + 1) * ATT_HD)
        vc = slice(kvw + kvh * ATT_HD, kvw + (kvh + 1) * ATT_HD)
        heads = [kvh * ATT_GRP + g for g in range(ATT_GRP)]
        k_src = [kvp_ref[:, kc], kvm_ref[:blk, kc], kvm_ref[blk:, kc], kvn_ref[:, kc]]
        k_rot = [_rope(k.astype(F32), t).astype(BF16) for k, t in zip(k_src, tabs)]
        v_src = [kvp_ref[:, vc], kvm_ref[:blk, vc], kvm_ref[blk:, vc], kvn_ref[:, vc]]
        for sub in range(2):
            rows = slice(sub * blk, (sub + 1) * blk)
            qs = jnp.concatenate([_rope(q_ref[rows, h * ATT_HD:(h + 1) * ATT_HD].astype(F32), tabs[1 + sub])
                                  for h in heads], axis=0).astype(BF16)
            k_all = jnp.concatenate([kvc_ref[:, kc]] + k_rot[sub:sub + 3], axis=0)
            v_all = jnp.concatenate([kvc_ref[:, vc]] + v_src[sub:sub + 3], axis=0)
            s = _nt(qs, k_all)
            ps, ls = [], []
            for g, h in enumerate(heads):
                sg = jnp.where(masks[sub], s[g * blk:(g + 1) * blk, :], -jnp.inf)
                sk = sink_ref[h]
                m = jnp.maximum(jnp.max(sg, axis=-1, keepdims=True), sk)
                p = jnp.exp(sg - m)
                ls.append(jnp.sum(p, axis=-1, keepdims=True) + jnp.exp(sk - m))
                ps.append(p.astype(BF16))
            out = jnp.dot(jnp.concatenate(ps, axis=0), v_all, preferred_element_type=F32)
            for g, h in enumerate(heads):
                hs = slice(h * ATT_HD, (h + 1) * ATT_HD)
                gate = g_ref[rows, hs].astype(F32)
                o_ref[rows, hs] = (out[g * blk:(g + 1) * blk, :] / ls[g] * _silu(gate)).astype(o_ref.dtype)


def _attention(p16, sink, rope_tab):
    blk = ATT_BLOCK
    qw = ATT_HQ * ATT_HD
    kvw = 2 * ATT_HKV * ATT_HD
    q_blk = _P16_OFF["wa_q"] // qw
    g_blk = _P16_OFF["wa_g"] // qw
    kv_blk = _P16_OFF["wa_k"] // kvw
    lat_q = R_LAT // Q_STEP

    def q_rows(b, s):
        return jnp.where(s < LAT_STEPS, LAT_STEPS * b + s, lat_q + CTX_STEPS * b + s - LAT_STEPS)

    def prev_blk(s):
        return jnp.clip(2 * s - 1, 0, LAT_BLKS - 1)

    def next_blk(s):
        return jnp.clip(2 * s + 2, 0, LAT_BLKS - 1)

    ins = [
        pl.BlockSpec(memory_space=pltpu.SMEM),
        pl.BlockSpec((Q_STEP, qw), lambda b, s: (q_rows(b, s), q_blk)),
        pl.BlockSpec((Q_STEP, qw), lambda b, s: (q_rows(b, s), g_blk)),
        pl.BlockSpec((CTX_LEN, kvw), lambda b, s: (R_LAT // CTX_LEN + b, kv_blk)),
        pl.BlockSpec((Q_STEP, kvw), lambda b, s: (q_rows(b, s), kv_blk)),
        pl.BlockSpec((blk, kvw), lambda b, s: (LAT_BLKS * b + prev_blk(s), kv_blk)),
        pl.BlockSpec((blk, kvw), lambda b, s: (LAT_BLKS * b + next_blk(s), kv_blk)),
        pl.BlockSpec((Q_STEP, 2 * ATT_HD), lambda b, s: (s, 0)),
        pl.BlockSpec((blk, 2 * ATT_HD), lambda b, s: (prev_blk(s), 0)),
        pl.BlockSpec((blk, 2 * ATT_HD), lambda b, s: (next_blk(s), 0)),
    ]
    return pl.pallas_call(
        _attn_kernel,
        grid=(BATCH, LAT_STEPS + CTX_STEPS),
        in_specs=ins,
        out_specs=pl.BlockSpec((Q_STEP, qw), lambda b, s: (q_rows(b, s), 0)),
        out_shape=jax.ShapeDtypeStruct((ROWS, qw), BF16),
        compiler_params=pltpu.CompilerParams(vmem_limit_bytes=VMEM_LIMIT),
        name="window_attn",
    )(sink, p16, p16, p16, p16, p16, p16, rope_tab, rope_tab, rope_tab)


def _head_norm(o, gain, heads, width):
    ys = []
    for h in range(heads):
        oh = o[:, h * width:(h + 1) * width]
        ys.append(oh * lax.rsqrt(jnp.mean(oh * oh, axis=-1, keepdims=True) + EPS) * gain)
    return jnp.concatenate(ys, axis=1)


def _merge_kernel(final, ogf_ref, ogb_ref, gg_ref, ohf_ref, ohb_ref, hg_ref, ya_ref, mg_ref, x_ref,
                  mod_ref, gn_gla_ref, gn_hg_ref, wbr_ref, wout_ref, *rest):
    o_gla = ogf_ref[...].astype(F32) + ogb_ref[...].astype(F32)
    o_hg = ohf_ref[...].astype(F32) + ohb_ref[...].astype(F32)
    y_gla = _head_norm(o_gla, gn_gla_ref[...], GLA_HEADS, GLA_HV) * _silu(gg_ref[...].astype(F32))
    y_hg = _head_norm(o_hg, gn_hg_ref[...], HGRN_HEADS, HGRN_HV) * _silu(hg_ref[...].astype(F32))
    ys = (y_gla.astype(BF16), y_hg.astype(BF16), ya_ref[...])
    merged = None
    for n in range(N_BRANCH):
        proj = jnp.dot(ys[n], wbr_ref[n], preferred_element_type=F32)
        term = jax.nn.sigmoid(mg_ref[:, n * D_MODEL:(n + 1) * D_MODEL].astype(F32)) * proj
        merged = term if merged is None else merged + term
    upd = jnp.dot(merged.astype(BF16), wout_ref[...], preferred_element_type=F32)
    x_new = x_ref[...] + mod_ref[0][:, 2 * D_MODEL:] * upd
    if final:
        fg_ref, out_ref = rest
        out_ref[...] = x_new * lax.rsqrt(jnp.mean(x_new * x_new, axis=-1, keepdims=True) + EPS) * fg_ref[...]
    else:
        ng_ref, nmod_ref, xo_ref, h_ref = rest
        xo_ref[...] = x_new
        h_ref[...] = _norm_mod(x_new, ng_ref[...], nmod_ref[0])


def _merge(final, context, og, oh, y_att, p16, x2d, mod_l, gn_gla, gn_hg, wbr, wout, tail):
    tm = CTX_TILE if context else LAT_TILE
    rows = x2d.shape[0]
    blk0 = R_LAT // tm if context else 0
    mod_row = _ctx_mod_row if context else _lat_mod_row

    def shared(width, col=0):
        return pl.BlockSpec((tm, width), lambda t: (blk0 + t, col))

    def whole(shape):
        return pl.BlockSpec(shape, lambda t: (0,) * len(shape), pipeline_mode=pl.Buffered(1))

    ins = [
        shared(1024), shared(1024), shared(1024, _P16_OFF["ga_g"] // 1024),
        shared(1024), shared(1024), shared(1024, _P16_OFF["hg_g"] // 1024),
        shared(1024), shared(3072, _P16_OFF["mg"] // 3072),
        pl.BlockSpec((tm, D_MODEL), lambda t: (t, 0)),
        pl.BlockSpec((1, 1, 3 * D_MODEL), lambda t: (mod_row(t), 0, 0)),
        whole((1, GLA_HV)), whole((1, HGRN_HV)),
        whole((N_BRANCH, D_MODEL, D_MODEL)), whole((D_MODEL, D_MODEL)),
        whole((1, D_MODEL)),
    ]
    args = [og[0], og[1], p16, oh[0], oh[1], p16, y_att, p16, x2d, mod_l,
            gn_gla.reshape(1, GLA_HV), gn_hg.reshape(1, HGRN_HV), wbr, wout, tail[0].reshape(1, D_MODEL)]
    row_spec = pl.BlockSpec((tm, D_MODEL), lambda t: (t, 0))
    if final:
        out_specs = row_spec
        out_shape = jax.ShapeDtypeStruct((rows, D_MODEL), F32)
    else:
        ins.append(pl.BlockSpec((1, 1, 3 * D_MODEL), lambda t: (mod_row(t), 0, 0)))
        args.append(tail[1])
        out_specs = [row_spec, row_spec]
        out_shape = [jax.ShapeDtypeStruct((rows, D_MODEL), F32), jax.ShapeDtypeStruct((rows, D_MODEL), BF16)]
    return pl.pallas_call(
        functools.partial(_merge_kernel, final),
        grid=(rows // tm,),
        in_specs=ins,
        out_specs=out_specs,
        out_shape=out_shape,
        compiler_params=pltpu.CompilerParams(vmem_limit_bytes=VMEM_LIMIT),
        name="merge_final" if final else ("merge_ctx" if context else "merge"),
    )(*args)


def _rope_table():
    r = ATT_HD // 4
    inv = ROPE_BASE ** (-jnp.arange(r, dtype=F32) / r)
    t = jnp.arange(SEQ)
    ang_row = (t // GRID_W).astype(F32)[:, None] * inv
    ang_col = (t % GRID_W).astype(F32)[:, None] * inv
    cos = jnp.concatenate([jnp.cos(ang_row)] * 2 + [jnp.cos(ang_col)] * 2, axis=1)
    sin = jnp.concatenate([-jnp.sin(ang_row), jnp.sin(ang_row), -jnp.sin(ang_col), jnp.sin(ang_col)], axis=1)
    tab = jnp.concatenate([cos, sin], axis=1)
    ident = jnp.concatenate([jnp.ones((CTX_LEN, ATT_HD), F32), jnp.zeros((CTX_LEN, ATT_HD), F32)], axis=1)
    return jnp.concatenate([tab, ident], axis=0)


def _cols(w, name):
    o, s = _IN_OFF[name]
    return w[:, o:o + s] * _Q_SCALE.get(name, 1.0)


def kernel(x, c, ctx, c_ctx, norm_g, w_ada, b_ada, w_in, gla_w_a2, gla_b_a2, gla_norm_g, hgrn_lb_logits,
           hgrn_norm_g, attn_sink, w_branch, w_out, final_g):
    cc = jnp.zeros((16, D_MODEL), F32).at[:BATCH].set(c).at[BATCH].set(c_ctx)
    mod = _modulation(cc, w_ada, b_ada).reshape(DEPTH, 16, 1, 3 * D_MODEL)
    rope_tab = _rope_table()
    lb_cum = jnp.cumsum(jax.nn.softmax(hgrn_lb_logits.astype(F32), axis=0), axis=0)
    lower_bounds = lb_cum - lb_cum[0]

    xl = x.reshape(R_LAT, D_MODEL)
    xc = ctx.reshape(R_CTX, D_MODEL)
    hl = _norm_modulate(xl, norm_g[0], mod[0], LAT_TILE, _lat_mod_row)
    hc = _norm_modulate(xc, norm_g[0], mod[0], CTX_TILE, _ctx_mod_row)
    out = None
    for l in range(DEPTH):
        final = l == DEPTH - 1
        w = w_in[l]
        w16 = jnp.concatenate([_cols(w, n) for n in _P16_ORDER], axis=1).astype(BF16)
        w32 = jnp.concatenate([_cols(w, "hg_f"), _cols(w, "ga_lr"),
                               jnp.zeros((D_MODEL, LR_PAD - 2 * GLA_RANK), F32)], axis=1).astype(BF16)
        wz = jnp.zeros((LR_PAD, 2 * GLA_HEADS * GLA_HK), F32)
        wz = wz.at[:GLA_RANK, :GLA_HEADS * GLA_HK].set(gla_w_a2[l, 0])
        wz = wz.at[GLA_RANK:2 * GLA_RANK, GLA_HEADS * GLA_HK:].set(gla_w_a2[l, 1]).astype(BF16)
        bz = gla_b_a2[l].reshape(1, 2 * GLA_HEADS * GLA_HK)
        lb = lower_bounds[l]
        lb_logs = jnp.stack([jnp.log(lb), jnp.log1p(-lb)], axis=0)

        p16 = _in_proj(hl, hc, w16, BF16, 512)
        p32 = _in_proj(hl, hc, w32, F32, 256)
        og = _gla(p16, p32, wz, bz)
        oh = _hgrn(p16, p32, lb_logs)
        y_att = _attention(p16, attn_sink[l], rope_tab)
        common = (og, oh, y_att, p16)
        params = (mod[l], gla_norm_g[l], hgrn_norm_g[l], w_branch[l].astype(BF16), w_out[l].astype(BF16))
        if final:
            out = _merge(True, False, *common, xl, *params, (final_g,))
        else:
            tail = (norm_g[l + 1], mod[l + 1])
            xl, hl = _merge(False, False, *common, xl, *params, tail)
            xc, hc = _merge(False, True, *common, xc, *params, tail)
    return out.reshape(BATCH, SEQ, D_MODEL)
```

```python
import functools

import numpy as np
import jax
import jax.numpy as jnp
from jax import lax
from jax.experimental import pallas as pl
from jax.experimental.pallas import tpu as pltpu

F32 = jnp.float32
BF16 = jnp.bfloat16

D_MODEL = 1024
BATCH = 8
SEQ = 4096
DEPTH = 2
CTX_LEN = 256
GRID_W = 64
N_BRANCH = 3
GLA_HEADS = 4
GLA_HV = 256
GLA_HK = 128
GLA_RANK = 16
GLA_TAU = 16.0
HGRN_HEADS = 8
HGRN_HV = 128
HGRN_HK = 128
HGRN_F = HGRN_HEADS * HGRN_HK
ATT_HD = 128
ATT_HQ = 8
ATT_HKV = 2
ATT_GRP = ATT_HQ // ATT_HKV
WINDOW = 128
ATT_BLOCK = 128
ROPE_BASE = 10000.0
EPS = 1e-6

R_LAT = BATCH * SEQ
R_CTX = BATCH * CTX_LEN
ROWS = R_LAT + R_CTX

_IN_SIZES = (512, 512, 1024, 1024, 32, 1024, 2048, 1024, 1024, 1024, 256, 256, 1024, 3072)
_IN_NAMES = ("ga_q", "ga_k", "ga_v", "ga_g", "ga_lr", "hg_q", "hg_f", "hg_i", "hg_g",
             "wa_q", "wa_k", "wa_v", "wa_g", "mg")
_IN_OFF = {}
_o = 0
for _n, _s in zip(_IN_NAMES, _IN_SIZES):
    _IN_OFF[_n] = (_o, _s)
    _o += _s

_P16_ORDER = ("mg", "ga_v", "ga_g", "hg_q", "hg_i", "hg_g", "wa_q", "wa_g", "ga_q", "ga_k", "wa_k", "wa_v")
_P16_OFF = {}
_o = 0
for _n in _P16_ORDER:
    _P16_OFF[_n] = _o
    _o += _IN_OFF[_n][1]
P16_COLS = _o
LR_PAD = 256
LOG2E = 1.4426950408889634
_Q_SCALE = {"ga_q": GLA_HK ** -0.5, "wa_q": ATT_HD ** -0.5 * LOG2E}

VMEM_LIMIT = 56 * 1024 * 1024


def _nt(a, b):
    return lax.dot_general(a, b, (((1,), (1,)), ((), ())), preferred_element_type=F32)


def _tn(a, b):
    return lax.dot_general(a, b, (((0,), (0,)), ((), ())), preferred_element_type=F32)


def _log_sigmoid(z):
    return jnp.minimum(z, 0.0) - jnp.log(1.0 + jnp.exp(-jnp.abs(z)))


def _silu(x):
    return x * jax.nn.sigmoid(x)


def _norm_mod(x, g, m):
    y = x * lax.rsqrt(jnp.mean(x * x, axis=-1, keepdims=True) + EPS) * g
    return (y * (1.0 + m[:, D_MODEL:2 * D_MODEL]) + m[:, :D_MODEL]).astype(BF16)


def _mod_kernel(c_ref, w_ref, b_ref, o_ref):
    a = _silu(c_ref[...])
    o_ref[0] = jnp.dot(a.astype(BF16), w_ref[0].astype(BF16), preferred_element_type=F32) + b_ref[0]


def _modulation(cc, w_ada, b_ada):
    tn = 512
    return pl.pallas_call(
        _mod_kernel,
        grid=(DEPTH, 3 * D_MODEL // tn),
        in_specs=[
            pl.BlockSpec((16, D_MODEL), lambda l, j: (0, 0)),
            pl.BlockSpec((1, D_MODEL, tn), lambda l, j: (l, 0, j)),
            pl.BlockSpec((1, 1, tn), lambda l, j: (l, 0, j)),
        ],
        out_specs=pl.BlockSpec((1, 16, tn), lambda l, j: (l, 0, j)),
        out_shape=jax.ShapeDtypeStruct((DEPTH, 16, 3 * D_MODEL), F32),
        name="modulation",
    )(cc, w_ada, b_ada.reshape(DEPTH, 1, 3 * D_MODEL))


LAT_TILE = 512
CTX_TILE = 256
MERGE_ROWS = 256


def _lat_mod_row(t):
    return t // (SEQ // LAT_TILE)


def _ctx_mod_row(t):
    return BATCH


def _norm_kernel(x_ref, g_ref, mod_ref, h_ref):
    h_ref[...] = _norm_mod(x_ref[...], g_ref[...], mod_ref[0])


def _norm_modulate(x2d, norm_g, mod_l, tm, mod_row):
    rows = x2d.shape[0]
    return pl.pallas_call(
        _norm_kernel,
        grid=(rows // tm,),
        in_specs=[
            pl.BlockSpec((tm, D_MODEL), lambda t: (t, 0)),
            pl.BlockSpec((1, D_MODEL), lambda t: (0, 0)),
            pl.BlockSpec((1, 1, 3 * D_MODEL), lambda t: (mod_row(t), 0, 0)),
        ],
        out_specs=pl.BlockSpec((tm, D_MODEL), lambda t: (t, 0)),
        out_shape=jax.ShapeDtypeStruct((rows, D_MODEL), BF16),
        name="norm_modulate",
    )(x2d, norm_g.reshape(1, D_MODEL), mod_l)


PROJ_TILE = 2048
N_LAT_PROJ = R_LAT // PROJ_TILE
N_CTX_PROJ = R_CTX // PROJ_TILE


def _mm_kernel(hl_ref, hc_ref, w_ref, o_ref):
    i = pl.program_id(0)

    @pl.when(i < N_LAT_PROJ)
    def _():
        o_ref[...] = jnp.dot(hl_ref[...], w_ref[...], preferred_element_type=F32).astype(o_ref.dtype)

    @pl.when(i >= N_LAT_PROJ)
    def _():
        o_ref[...] = jnp.dot(hc_ref[...], w_ref[...], preferred_element_type=F32).astype(o_ref.dtype)


def _proj_specs(k, tn):
    tm = PROJ_TILE
    return [
        pl.BlockSpec((tm, k), lambda i, j: (jnp.minimum(i, N_LAT_PROJ - 1), 0)),
        pl.BlockSpec((tm, k), lambda i, j: (jnp.maximum(i - N_LAT_PROJ, 0), 0)),
        pl.BlockSpec((k, tn), lambda i, j: (0, j)),
    ]


def _in_proj(hl, hc, w, out_dtype, tn):
    k, n = w.shape
    return pl.pallas_call(
        _mm_kernel,
        grid=(N_LAT_PROJ + N_CTX_PROJ, n // tn),
        in_specs=_proj_specs(k, tn),
        out_specs=pl.BlockSpec((PROJ_TILE, tn), lambda i, j: (i, j)),
        out_shape=jax.ShapeDtypeStruct((ROWS, n), out_dtype),
        compiler_params=pltpu.CompilerParams(vmem_limit_bytes=VMEM_LIMIT),
        name="in_proj",
    )(hl, hc, w)


GATE_ROWS = 256


def _forget_proj_kernel(zero_bound, hl_ref, hc_ref, w_ref, lb_ref, o_ref):
    i = pl.program_id(0)
    a = lb_ref[0:1, :]
    l1m = lb_ref[1:2, :]

    def run(h_ref):
        for r in range(PROJ_TILE // GATE_ROWS):
            rows = slice(r * GATE_ROWS, (r + 1) * GATE_ROWS)
            ls = _log_sigmoid(jnp.dot(h_ref[rows, :], w_ref[...], preferred_element_type=F32))
            if zero_bound:
                o_ref[rows, :] = ls
            else:
                c = l1m + ls
                o_ref[rows, :] = jnp.maximum(a, c) + jnp.log(1.0 + jnp.exp(-jnp.abs(a - c)))

    @pl.when(i < N_LAT_PROJ)
    def _():
        run(hl_ref)

    @pl.when(i >= N_LAT_PROJ)
    def _():
        run(hc_ref)


def _forget_proj(hl, hc, w, lb_logs, zero_bound):
    k, n = w.shape
    tn = 256
    return pl.pallas_call(
        functools.partial(_forget_proj_kernel, zero_bound),
        grid=(N_LAT_PROJ + N_CTX_PROJ, n // tn),
        in_specs=_proj_specs(k, tn) + [pl.BlockSpec((2, tn), lambda i, j: (0, j))],
        out_specs=pl.BlockSpec((PROJ_TILE, tn), lambda i, j: (i, j)),
        out_shape=jax.ShapeDtypeStruct((ROWS, n), F32),
        compiler_params=pltpu.CompilerParams(vmem_limit_bytes=VMEM_LIMIT),
        name="forget_proj",
    )(hl, hc, w, lb_logs)


CHUNK = 128
SUB = 16
N_SUB = CHUNK // SUB
STEP_CHUNKS = 2
STEP_ROWS = STEP_CHUNKS * CHUNK
LAT_CHUNKS = SEQ // STEP_ROWS
CTX_CHUNKS = CTX_LEN // STEP_ROWS
N_CHUNK = LAT_CHUNKS + CTX_CHUNKS
CTX_CHUNK0 = R_LAT // STEP_ROWS

_PAIR_LEVELS = tuple(SUB * 2 ** i for i in range(N_SUB.bit_length() - 1))
_GROUPS = tuple(2 ** (i + 1) for i in range(N_SUB.bit_length() - 1))
_FAC_ROWS = -(-8 * (2 * len(_GROUPS) + 1) // 16) * 16


def _fwd_rows(b, n):
    return jnp.where(n < CTX_CHUNKS, CTX_CHUNK0 + CTX_CHUNKS * b + n, LAT_CHUNKS * b + n - CTX_CHUNKS)


def _bwd_rows(b, n):
    return jnp.where(n < CTX_CHUNKS, CTX_CHUNK0 + CTX_CHUNKS * b + CTX_CHUNKS - 1 - n,
                     LAT_CHUNKS * b + N_CHUNK - 1 - n)


def _level_matrix(reverse):
    nb = N_SUB
    pos = [nb - 1 - m for m in range(nb)] if reverse else list(range(nb))
    mat = np.zeros((_FAC_ROWS, 16), np.float32)
    for gi, r in enumerate(_GROUPS):
        for m in range(nb):
            for m2 in range(nb):
                if m2 // r == m // r:
                    if pos[m2] < pos[m]:
                        mat[8 * gi + m, m2] = 1.0
                    if pos[m2] > pos[m]:
                        mat[8 * (len(_GROUPS) + gi) + m, m2] = 1.0
    mat[8 * 2 * len(_GROUPS), :nb] = 1.0
    return mat


def _cumsum_matrix(reverse):
    i = np.arange(CHUNK)[:, None]
    j = np.arange(CHUNK)[None, :]
    t = ((i // SUB == j // SUB) & ((j >= i) if reverse else (j <= i))).astype(np.float32)
    return np.concatenate([t, t], axis=1)


def _local_cumsum(x, reverse):
    n = x.shape[0]
    r8 = jnp.bitwise_and(lax.broadcasted_iota(jnp.int32, x.shape, 0), 7)
    for d in (1, 2, 4):
        if reverse:
            x = x + jnp.where(r8 < 8 - d, pltpu.roll(x, n - d, 0), 0.0)
        else:
            x = x + jnp.where(r8 >= d, pltpu.roll(x, d, 0), 0.0)
    out = []
    for m in range(n // SUB):
        lo = x[SUB * m:SUB * m + 8]
        hi = x[SUB * m + 8:SUB * m + SUB]
        if reverse:
            lo = lo + hi[0:1]
        else:
            hi = hi + lo[7:8]
        out += [lo, hi]
    return jnp.concatenate(out, axis=0)


def _rep_rows(x):
    return jnp.concatenate([jnp.broadcast_to(x[m:m + 1], (SUB, x.shape[1])) for m in range(N_SUB)], axis=0)


def _split3(x):
    hi = x.astype(BF16)
    r1 = x - hi.astype(F32)
    mid = r1.astype(BF16)
    lo = (r1 - mid.astype(F32)).astype(BF16)
    return hi, mid, lo


def _decay_factors(q, k, la, cmat, lmat, reverse):
    if cmat is None:
        bl = _local_cumsum(la, reverse)
    else:
        hi = la.astype(BF16)
        lo = (la - hi.astype(F32)).astype(BF16)
        bl = jnp.dot(cmat, jnp.concatenate([hi, lo], axis=0), preferred_element_type=F32)
    last = [SUB * m + (0 if reverse else SUB - 1) for m in range(N_SUB)]
    tot = jnp.concatenate([bl[i:i + 1] for i in last], axis=0)
    tot16 = jnp.concatenate([tot, jnp.zeros((16 - N_SUB, tot.shape[1]), F32)], axis=0)
    logs = sum(jnp.dot(lmat, t, preferred_element_type=F32) for t in _split3(tot16))
    fac = jnp.exp(logs)
    qe = q * jnp.exp(bl)
    ke = k * jnp.exp(_rep_rows(tot) - bl)
    q_lv = {SUB: qe.astype(BF16)}
    k_lv = {SUB: ke.astype(BF16), "diag": (k * jnp.exp(-bl)).astype(BF16)}
    for gi in range(len(_GROUPS)):
        s = 2 * SUB * 2 ** gi
        q_lv[s] = (qe * _rep_rows(fac[8 * gi:8 * gi + N_SUB])).astype(BF16)
        fo = 8 * (len(_GROUPS) + gi)
        k_lv[s] = (ke * _rep_rows(fac[fo:fo + N_SUB])).astype(BF16)
    decay = fac[8 * 2 * len(_GROUPS):8 * 2 * len(_GROUPS) + 1]
    return q_lv, k_lv, decay


def _lead_rows(x, s, reverse):
    first = 0 if reverse else 1
    pieces = [x[(2 * g + first) * s:(2 * g + first + 1) * s] for g in range(CHUNK // (2 * s))]
    return pieces[0] if len(pieces) == 1 else jnp.concatenate(pieces, axis=0)


def _scan_masks(reverse):
    n = CHUNK
    i = lax.broadcasted_iota(jnp.int32, (n, n), 0)
    j = lax.broadcasted_iota(jnp.int32, (n, n), 1)
    sh = SUB.bit_length() - 1
    same = jnp.right_shift(i, sh) == jnp.right_shift(j, sh)
    masks = {"diag": jnp.logical_and(same, j >= i if reverse else j <= i)}
    rc = lax.broadcasted_iota(jnp.int32, (n // 2, n), 0)
    jc = lax.broadcasted_iota(jnp.int32, (n // 2, n), 1)
    for s in _PAIR_LEVELS:
        sh = s.bit_length() - 1
        masks[s] = jnp.right_shift(jc, sh) == 2 * jnp.right_shift(rc, sh) + (1 if reverse else 0)
    return masks


def _intra_scores(q_lv, k_lv, cols, masks, reverse):
    diag = jnp.where(masks["diag"], _nt(q_lv[SUB][:, cols], k_lv["diag"][:, cols]), 0.0)
    lead = {}
    for s in _PAIR_LEVELS:
        lead[s] = jnp.where(masks[s], _nt(_lead_rows(q_lv[s][:, cols], s, reverse), k_lv[s][:, cols]), 0.0)
    blocks = []
    for m in range(N_SUB):
        blk = diag[SUB * m:SUB * (m + 1)]
        for s in _PAIR_LEVELS:
            r = s // SUB
            ms = m // r
            if ms % 2 == (0 if reverse else 1):
                off = (ms // 2) * s + (m % r) * SUB
                blk = blk + lead[s][off:off + SUB]
        blocks.append(blk)
    return jnp.concatenate(blocks, axis=0).astype(BF16)


def _step_chunks(reverse):
    order = range(STEP_CHUNKS - 1, -1, -1) if reverse else range(STEP_CHUNKS)
    return [slice(c * CHUNK, (c + 1) * CHUNK) for c in order]


def _scan_step(directions, heads, hk, hv):
    plans = []
    for factors_of, v_ref, o_ref, st_ref, reverse in directions:
        for rows in _step_chunks(reverse):
            plans.append((factors_of(rows), rows, v_ref, o_ref, st_ref, reverse))
    masks = {reverse: _scan_masks(reverse) for _, _, _, _, reverse in directions}
    atts = [[_intra_scores(q_lv, k_lv, slice(h * hk, (h + 1) * hk), masks[reverse], reverse) for h in range(heads)]
            for (q_lv, k_lv, _), _, _, _, _, reverse in plans]
    for ((q_lv, k_lv, decay), rows, v_ref, o_ref, st_ref, _), att in zip(plans, atts):
        for h in range(heads):
            ks = slice(h * hk, (h + 1) * hk)
            vs = slice(h * hv, (h + 1) * hv)
            v = v_ref[rows, vs]
            st = st_ref[h]
            o_ref[rows, vs] = (jnp.dot(att[h], v, preferred_element_type=F32)
                               + _nt(q_lv[CHUNK][:, ks], st.astype(BF16))).astype(o_ref.dtype)
            st_ref[h] = decay[:, ks] * st + _tn(v, k_lv[CHUNK][:, ks])


def _reset_state(stf_ref, stb_ref):
    @pl.when(pl.program_id(1) == 0)
    def _():
        stf_ref[...] = jnp.zeros_like(stf_ref)
        stb_ref[...] = jnp.zeros_like(stb_ref)


def _gla_kernel(qf_ref, kf_ref, vf_ref, lrf_ref, qb_ref, kb_ref, vb_ref, lrb_ref, wz_ref, bz_ref,
                lmat_ref, of_ref, ob_ref, stf_ref, stb_ref):
    _reset_state(stf_ref, stb_ref)
    width = GLA_HEADS * GLA_HK
    gates = [jnp.dot(lr_ref[...].astype(BF16), wz_ref[:, d * width:(d + 1) * width], preferred_element_type=F32)
             + bz_ref[:, d * width:(d + 1) * width] for d, lr_ref in enumerate((lrf_ref, lrb_ref))]
    directions = []
    for d, (q_ref, k_ref, v_ref, o_ref, st_ref) in enumerate((
            (qf_ref, kf_ref, vf_ref, of_ref, stf_ref),
            (qb_ref, kb_ref, vb_ref, ob_ref, stb_ref))):
        reverse = d == 1

        def factors_of(rows, d=d, q_ref=q_ref, k_ref=k_ref, reverse=reverse):
            la = _log_sigmoid(gates[d][rows]) / GLA_TAU
            return _decay_factors(q_ref[rows, :].astype(F32), k_ref[rows, :].astype(F32), la, None,
                                  lmat_ref[d], reverse)

        directions.append((factors_of, v_ref, o_ref, st_ref, reverse))
    _scan_step(directions, GLA_HEADS, GLA_HK, GLA_HV)


def _scan_call(kernel_fn, ins, args, name, heads, hk, hv, mxu_cumsum):
    out_sds = jax.ShapeDtypeStruct((ROWS, heads * hv), BF16)
    consts = []
    if mxu_cumsum:
        consts.append(np.stack([_cumsum_matrix(False), _cumsum_matrix(True)]))
    consts.append(np.stack([_level_matrix(False), _level_matrix(True)]))
    return pl.pallas_call(
        kernel_fn,
        grid=(BATCH, N_CHUNK),
        in_specs=ins + [pl.BlockSpec(c.shape, lambda b, n: (0, 0, 0)) for c in consts],
        out_specs=[pl.BlockSpec((STEP_ROWS, heads * hv), lambda b, n: (_fwd_rows(b, n), 0)),
                   pl.BlockSpec((STEP_ROWS, heads * hv), lambda b, n: (_bwd_rows(b, n), 0))],
        out_shape=[out_sds, out_sds],
        scratch_shapes=[pltpu.VMEM((heads, hv, hk), F32), pltpu.VMEM((heads, hv, hk), F32)],
        compiler_params=pltpu.CompilerParams(dimension_semantics=("arbitrary", "arbitrary"),
                                             vmem_limit_bytes=VMEM_LIMIT),
        name=name,
    )(*args, *[jnp.asarray(c, dtype=BF16) for c in consts])


def _chunk_spec(width, blk, rows_of):
    return pl.BlockSpec((STEP_ROWS, width), lambda b, n: (rows_of(b, n), blk))


def _gla(p16, p_lr, wz, bz):
    q_blk = _P16_OFF["ga_q"] // 512
    k_blk = _P16_OFF["ga_k"] // 512
    v_blk = _P16_OFF["ga_v"] // 1024
    ins, args = [], []
    for rows_of in (_fwd_rows, _bwd_rows):
        ins += [_chunk_spec(512, q_blk, rows_of), _chunk_spec(512, k_blk, rows_of),
                _chunk_spec(1024, v_blk, rows_of), _chunk_spec(LR_PAD, 0, rows_of)]
        args += [p16, p16, p16, p_lr]
    ins += [pl.BlockSpec((LR_PAD, 1024), lambda b, n: (0, 0)), pl.BlockSpec((1, 1024), lambda b, n: (0, 0))]
    args += [wz, bz]
    return _scan_call(_gla_kernel, ins, args, "gla_scan", GLA_HEADS, GLA_HK, GLA_HV, False)


def _hgrn_kernel(qf_ref, if_ref, ff_ref, qb_ref, ib_ref, fb_ref, cmat_ref, lmat_ref, of_ref, ob_ref,
                 stf_ref, stb_ref):
    _reset_state(stf_ref, stb_ref)
    directions = []
    for d, (q_ref, i_ref, f_ref, o_ref, st_ref) in enumerate((
            (qf_ref, if_ref, ff_ref, of_ref, stf_ref),
            (qb_ref, ib_ref, fb_ref, ob_ref, stb_ref))):
        reverse = d == 1

        def factors_of(rows, d=d, q_ref=q_ref, f_ref=f_ref, reverse=reverse):
            log_f = f_ref[rows, :]
            return _decay_factors(q_ref[rows, :].astype(F32), 1.0 - jnp.exp(log_f), log_f, cmat_ref[d],
                                  lmat_ref[d], reverse)

        directions.append((factors_of, i_ref, o_ref, st_ref, reverse))
    _scan_step(directions, HGRN_HEADS, HGRN_HK, HGRN_HV)


def _hgrn(p16, log_f):
    q_blk = _P16_OFF["hg_q"] // 1024
    i_blk = _P16_OFF["hg_i"] // 1024
    ins = [_chunk_spec(1024, q_blk, _fwd_rows), _chunk_spec(1024, i_blk, _fwd_rows), _chunk_spec(1024, 0, _fwd_rows),
           _chunk_spec(1024, q_blk, _bwd_rows), _chunk_spec(1024, i_blk, _bwd_rows), _chunk_spec(1024, 1, _bwd_rows)]
    args = [p16, p16, log_f, p16, p16, log_f]
    return _scan_call(_hgrn_kernel, ins, args, "hgrn_scan", HGRN_HEADS, HGRN_HK, HGRN_HV, True)


Q_STEP = 2 * ATT_BLOCK
LAT_BLKS = SEQ // ATT_BLOCK
LAT_STEPS = SEQ // Q_STEP
CTX_STEPS = CTX_LEN // Q_STEP
_FAR = 4 * ATT_BLOCK


def _rope(x, tab):
    lane = lax.broadcasted_iota(jnp.int32, x.shape, 1)
    partner = jnp.where(jnp.bitwise_and(lane, 32) == 0, pltpu.roll(x, 96, 1), pltpu.roll(x, 32, 1))
    return x * tab[:, :ATT_HD] + partner * tab[:, ATT_HD:]


def _window_mask(jblk, is_lat):
    blk = ATT_BLOCK
    off_prev = jnp.where(jnp.logical_and(is_lat, jblk >= 1), 0, _FAR)
    off_cur = jnp.where(is_lat, 0, _FAR)
    off_next = jnp.where(jnp.logical_and(is_lat, jblk + 1 < LAT_BLKS), 0, _FAR)
    qi = lax.broadcasted_iota(jnp.int32, (blk, 3 * blk), 0)
    col = lax.broadcasted_iota(jnp.int32, (blk, 3 * blk), 1)
    land, lor = jnp.logical_and, jnp.logical_or
    in_prev = land(col < blk, col >= qi + off_prev)
    in_cur = land(col >= blk + off_cur, col < 2 * blk)
    in_next = land(col >= 2 * blk, col - 2 * blk + off_next <= qi)
    return lor(in_prev, lor(in_cur, in_next))


def _attn_kernel(sink_ref, q_ref, g_ref, kvc_ref, kvm_ref, kvp_ref, kvn_ref, tm_ref, tp_ref, tn_ref, o_ref):
    step = pl.program_id(1)
    is_lat = step < LAT_STEPS
    blk = ATT_BLOCK
    kvw = ATT_HKV * ATT_HD
    tmid = tm_ref[...]
    tabs = [tp_ref[...], tmid[:blk], tmid[blk:], tn_ref[...]]
    masks = [_window_mask(2 * step + sub, is_lat) for sub in range(2)]
    for kvh in range(ATT_HKV):
        kc = slice(kvh * ATT_HD, (kvh + 1) * ATT_HD)
        vc = slice(kvw + kvh * ATT_HD, kvw + (kvh + 1) * ATT_HD)
        heads = [kvh * ATT_GRP + g for g in range(ATT_GRP)]
        k_src = [kvp_ref[:, kc], kvm_ref[:blk, kc], kvm_ref[blk:, kc], kvn_ref[:, kc]]
        k_rot = [_rope(k.astype(F32), t).astype(BF16) for k, t in zip(k_src, tabs)]
        v_src = [kvp_ref[:, vc], kvm_ref[:blk, vc], kvm_ref[blk:, vc], kvn_ref[:, vc]]
        for sub in range(2):
            rows = slice(sub * blk, (sub + 1) * blk)
            qs = jnp.concatenate([_rope(q_ref[rows, h * ATT_HD:(h + 1) * ATT_HD].astype(F32), tabs[1 + sub])
                                  for h in heads], axis=0).astype(BF16)
            k_all = jnp.concatenate([kvc_ref[:, kc]] + k_rot[sub:sub + 3], axis=0)
            v_all = jnp.concatenate([kvc_ref[:, vc]] + v_src[sub:sub + 3], axis=0)
            s = _nt(qs, k_all)
            ps, ls = [], []
            for g, h in enumerate(heads):
                s_ctx = s[g * blk:(g + 1) * blk, :CTX_LEN]
                s_loc = jnp.where(masks[sub], s[g * blk:(g + 1) * blk, CTX_LEN:], -jnp.inf)
                sk = sink_ref[h] * LOG2E
                m = jnp.maximum(jnp.maximum(jnp.max(s_ctx, axis=-1, keepdims=True),
                                            jnp.max(s_loc, axis=-1, keepdims=True)), sk)
                p = jnp.concatenate([jnp.exp2(s_ctx - m), jnp.exp2(s_loc - m)], axis=1)
                ls.append(jnp.sum(p, axis=-1, keepdims=True) + jnp.exp2(sk - m))
                ps.append(p.astype(BF16))
            out = jnp.dot(jnp.concatenate(ps, axis=0), v_all, preferred_element_type=F32)
            for g, h in enumerate(heads):
                hs = slice(h * ATT_HD, (h + 1) * ATT_HD)
                gate = _silu(g_ref[rows, hs].astype(F32))
                o_ref[rows, hs] = (out[g * blk:(g + 1) * blk, :] * (1.0 / ls[g]) * gate).astype(o_ref.dtype)


def _attention(p16, sink, rope_tab):
    blk = ATT_BLOCK
    qw = ATT_HQ * ATT_HD
    kvw = 2 * ATT_HKV * ATT_HD
    q_blk = _P16_OFF["wa_q"] // qw
    g_blk = _P16_OFF["wa_g"] // qw
    kv_blk = _P16_OFF["wa_k"] // kvw
    lat_q = R_LAT // Q_STEP

    def q_rows(b, s):
        return jnp.where(s < LAT_STEPS, LAT_STEPS * b + s, lat_q + CTX_STEPS * b + s - LAT_STEPS)

    def prev_blk(s):
        return jnp.clip(2 * s - 1, 0, LAT_BLKS - 1)

    def next_blk(s):
        return jnp.clip(2 * s + 2, 0, LAT_BLKS - 1)

    ins = [
        pl.BlockSpec(memory_space=pltpu.SMEM),
        pl.BlockSpec((Q_STEP, qw), lambda b, s: (q_rows(b, s), q_blk)),
        pl.BlockSpec((Q_STEP, qw), lambda b, s: (q_rows(b, s), g_blk)),
        pl.BlockSpec((CTX_LEN, kvw), lambda b, s: (R_LAT // CTX_LEN + b, kv_blk)),
        pl.BlockSpec((Q_STEP, kvw), lambda b, s: (q_rows(b, s), kv_blk)),
        pl.BlockSpec((blk, kvw), lambda b, s: (LAT_BLKS * b + prev_blk(s), kv_blk)),
        pl.BlockSpec((blk, kvw), lambda b, s: (LAT_BLKS * b + next_blk(s), kv_blk)),
        pl.BlockSpec((Q_STEP, 2 * ATT_HD), lambda b, s: (s, 0)),
        pl.BlockSpec((blk, 2 * ATT_HD), lambda b, s: (prev_blk(s), 0)),
        pl.BlockSpec((blk, 2 * ATT_HD), lambda b, s: (next_blk(s), 0)),
    ]
    return pl.pallas_call(
        _attn_kernel,
        grid=(BATCH, LAT_STEPS + CTX_STEPS),
        in_specs=ins,
        out_specs=pl.BlockSpec((Q_STEP, qw), lambda b, s: (q_rows(b, s), 0)),
        out_shape=jax.ShapeDtypeStruct((ROWS, qw), BF16),
        compiler_params=pltpu.CompilerParams(vmem_limit_bytes=VMEM_LIMIT),
        name="window_attn",
    )(sink, p16, p16, p16, p16, p16, p16, rope_tab, rope_tab, rope_tab)


def _head_norm(o, gain, heads, width):
    ys = []
    for h in range(heads):
        oh = o[:, h * width:(h + 1) * width]
        ys.append(oh * lax.rsqrt(jnp.mean(oh * oh, axis=-1, keepdims=True) + EPS) * gain)
    return jnp.concatenate(ys, axis=1)


def _merge_kernel(final, ogf_ref, ogb_ref, gg_ref, ohf_ref, ohb_ref, hg_ref, ya_ref, mg_ref, x_ref,
                  mod_ref, gn_gla_ref, gn_hg_ref, wbr_ref, wout_ref, *rest):
    for r in range(x_ref.shape[0] // MERGE_ROWS):
        rows = slice(r * MERGE_ROWS, (r + 1) * MERGE_ROWS)
        o_gla = ogf_ref[rows, :].astype(F32) + ogb_ref[rows, :].astype(F32)
        o_hg = ohf_ref[rows, :].astype(F32) + ohb_ref[rows, :].astype(F32)
        y_gla = _head_norm(o_gla, gn_gla_ref[...], GLA_HEADS, GLA_HV) * _silu(gg_ref[rows, :].astype(F32))
        y_hg = _head_norm(o_hg, gn_hg_ref[...], HGRN_HEADS, HGRN_HV) * _silu(hg_ref[rows, :].astype(F32))
        ys = (y_gla.astype(BF16), y_hg.astype(BF16), ya_ref[rows, :])
        merged = None
        for n in range(N_BRANCH):
            proj = jnp.dot(ys[n], wbr_ref[n], preferred_element_type=F32)
            term = jax.nn.sigmoid(mg_ref[rows, n * D_MODEL:(n + 1) * D_MODEL].astype(F32)) * proj
            merged = term if merged is None else merged + term
        upd = jnp.dot(merged.astype(BF16), wout_ref[...], preferred_element_type=F32)
        x_new = x_ref[rows, :] + mod_ref[0][:, 2 * D_MODEL:] * upd
        if final:
            fg_ref, out_ref = rest
            out_ref[rows, :] = (x_new * lax.rsqrt(jnp.mean(x_new * x_new, axis=-1, keepdims=True) + EPS)
                                * fg_ref[...])
        else:
            ng_ref, nmod_ref, xo_ref, h_ref = rest
            xo_ref[rows, :] = x_new
            h_ref[rows, :] = _norm_mod(x_new, ng_ref[...], nmod_ref[0])


def _merge(final, context, og, oh, y_att, p16, x2d, mod_l, gn_gla, gn_hg, wbr, wout, tail):
    tm = CTX_TILE if context else LAT_TILE
    rows = x2d.shape[0]
    blk0 = R_LAT // tm if context else 0
    mod_row = _ctx_mod_row if context else _lat_mod_row

    def shared(width, col=0):
        return pl.BlockSpec((tm, width), lambda t: (blk0 + t, col))

    def whole(shape):
        return pl.BlockSpec(shape, lambda t: (0,) * len(shape), pipeline_mode=pl.Buffered(1))

    ins = [
        shared(1024), shared(1024), shared(1024, _P16_OFF["ga_g"] // 1024),
        shared(1024), shared(1024), shared(1024, _P16_OFF["hg_g"] // 1024),
        shared(1024), shared(3072, _P16_OFF["mg"] // 3072),
        pl.BlockSpec((tm, D_MODEL), lambda t: (t, 0)),
        pl.BlockSpec((1, 1, 3 * D_MODEL), lambda t: (mod_row(t), 0, 0)),
        whole((1, GLA_HV)), whole((1, HGRN_HV)),
        whole((N_BRANCH, D_MODEL, D_MODEL)), whole((D_MODEL, D_MODEL)),
        whole((1, D_MODEL)),
    ]
    args = [og[0], og[1], p16, oh[0], oh[1], p16, y_att, p16, x2d, mod_l,
            gn_gla.reshape(1, GLA_HV), gn_hg.reshape(1, HGRN_HV), wbr, wout, tail[0].reshape(1, D_MODEL)]
    row_spec = pl.BlockSpec((tm, D_MODEL), lambda t: (t, 0))
    if final:
        out_specs = row_spec
        out_shape = jax.ShapeDtypeStruct((rows, D_MODEL), F32)
    else:
        ins.append(pl.BlockSpec((1, 1, 3 * D_MODEL), lambda t: (mod_row(t), 0, 0)))
        args.append(tail[1])
        out_specs = [row_spec, row_spec]
        out_shape = [jax.ShapeDtypeStruct((rows, D_MODEL), F32), jax.ShapeDtypeStruct((rows, D_MODEL), BF16)]
    return pl.pallas_call(
        functools.partial(_merge_kernel, final),
        grid=(rows // tm,),
        in_specs=ins,
        out_specs=out_specs,
        out_shape=out_shape,
        compiler_params=pltpu.CompilerParams(vmem_limit_bytes=VMEM_LIMIT),
        name="merge_final" if final else ("merge_ctx" if context else "merge"),
    )(*args)


def _rope_table():
    r = ATT_HD // 4
    inv = ROPE_BASE ** (-jnp.arange(r, dtype=F32) / r)
    t = jnp.arange(SEQ)
    ang_row = (t // GRID_W).astype(F32)[:, None] * inv
    ang_col = (t % GRID_W).astype(F32)[:, None] * inv
    cos = jnp.concatenate([jnp.cos(ang_row)] * 2 + [jnp.cos(ang_col)] * 2, axis=1)
    sin = jnp.concatenate([-jnp.sin(ang_row), jnp.sin(ang_row), -jnp.sin(ang_col), jnp.sin(ang_col)], axis=1)
    tab = jnp.concatenate([cos, sin], axis=1)
    ident = jnp.concatenate([jnp.ones((CTX_LEN, ATT_HD), F32), jnp.zeros((CTX_LEN, ATT_HD), F32)], axis=1)
    return jnp.concatenate([tab, ident], axis=0)


def _cols(w, name):
    o, s = _IN_OFF[name]
    return w[:, o:o + s] * _Q_SCALE.get(name, 1.0)


def kernel(x, c, ctx, c_ctx, norm_g, w_ada, b_ada, w_in, gla_w_a2, gla_b_a2, gla_norm_g, hgrn_lb_logits,
           hgrn_norm_g, attn_sink, w_branch, w_out, final_g):
    cc = jnp.zeros((16, D_MODEL), F32).at[:BATCH].set(c).at[BATCH].set(c_ctx)
    mod = _modulation(cc, w_ada, b_ada).reshape(DEPTH, 16, 1, 3 * D_MODEL)
    rope_tab = _rope_table()
    lb_cum = jnp.cumsum(jax.nn.softmax(hgrn_lb_logits.astype(F32), axis=0), axis=0)
    lower_bounds = lb_cum - lb_cum[0]

    xl = x.reshape(R_LAT, D_MODEL)
    xc = ctx.reshape(R_CTX, D_MODEL)
    hl = _norm_modulate(xl, norm_g[0], mod[0], LAT_TILE, _lat_mod_row)
    hc = _norm_modulate(xc, norm_g[0], mod[0], CTX_TILE, _ctx_mod_row)
    out = None
    for l in range(DEPTH):
        final = l == DEPTH - 1
        w = w_in[l]
        w16 = jnp.concatenate([_cols(w, n) for n in _P16_ORDER], axis=1).astype(BF16)
        w_f = _cols(w, "hg_f").astype(BF16)
        w_lr = jnp.concatenate([_cols(w, "ga_lr"), jnp.zeros((D_MODEL, LR_PAD - 2 * GLA_RANK), F32)],
                               axis=1).astype(BF16)
        wz = jnp.zeros((LR_PAD, 2 * GLA_HEADS * GLA_HK), F32)
        wz = wz.at[:GLA_RANK, :GLA_HEADS * GLA_HK].set(gla_w_a2[l, 0])
        wz = wz.at[GLA_RANK:2 * GLA_RANK, GLA_HEADS * GLA_HK:].set(gla_w_a2[l, 1]).astype(BF16)
        bz = gla_b_a2[l].reshape(1, 2 * GLA_HEADS * GLA_HK)
        lb = lower_bounds[l]
        lb_logs = jnp.stack([jnp.log(lb), jnp.log1p(-lb)], axis=0).reshape(2, 2 * HGRN_F)

        p16 = _in_proj(hl, hc, w16, BF16, 512)
        p_lr = _in_proj(hl, hc, w_lr, F32, LR_PAD)
        log_f = _forget_proj(hl, hc, w_f, lb_logs, l == 0)
        og = _gla(p16, p_lr, wz, bz)
        oh = _hgrn(p16, log_f)
        y_att = _attention(p16, attn_sink[l], rope_tab)
        common = (og, oh, y_att, p16)
        params = (mod[l], gla_norm_g[l], hgrn_norm_g[l], w_branch[l].astype(BF16), w_out[l].astype(BF16))
        if final:
            out = _merge(True, False, *common, xl, *params, (final_g,))
        else:
            tail = (norm_g[l + 1], mod[l + 1])
            xl, hl = _merge(False, False, *common, xl, *params, tail)
            xc, hc = _merge(False, True, *common, xc, *params, tail)
    return out.reshape(BATCH, SEQ, D_MODEL)
```

```python
import functools

import numpy as np
import jax
import jax.numpy as jnp
from jax import lax
from jax.experimental import pallas as pl
from jax.experimental.pallas import tpu as pltpu

F32 = jnp.float32
BF16 = jnp.bfloat16

D_MODEL = 1024
BATCH = 8
SEQ = 4096
DEPTH = 2
CTX_LEN = 256
GRID_W = 64
N_BRANCH = 3
GLA_HEADS = 4
GLA_HV = 256
GLA_HK = 128
GLA_RANK = 16
GLA_TAU = 16.0
HGRN_HEADS = 8
HGRN_HV = 128
HGRN_HK = 128
HGRN_F = HGRN_HEADS * HGRN_HK
ATT_HD = 128
ATT_HQ = 8
ATT_HKV = 2
ATT_GRP = ATT_HQ // ATT_HKV
WINDOW = 128
ATT_BLOCK = 128
ROPE_BASE = 10000.0
EPS = 1e-6

R_LAT = BATCH * SEQ
R_CTX = BATCH * CTX_LEN
ROWS = R_LAT + R_CTX

_IN_SIZES = (512, 512, 1024, 1024, 32, 1024, 2048, 1024, 1024, 1024, 256, 256, 1024, 3072)
_IN_NAMES = ("ga_q", "ga_k", "ga_v", "ga_g", "ga_lr", "hg_q", "hg_f", "hg_i", "hg_g",
             "wa_q", "wa_k", "wa_v", "wa_g", "mg")
_IN_OFF = {}
_o = 0
for _n, _s in zip(_IN_NAMES, _IN_SIZES):
    _IN_OFF[_n] = (_o, _s)
    _o += _s

_P16_ORDER = ("mg", "ga_v", "ga_g", "hg_q", "hg_i", "hg_g", "wa_q", "wa_g", "ga_q", "ga_k", "wa_k", "wa_v")
_P16_OFF = {}
_o = 0
for _n in _P16_ORDER:
    _P16_OFF[_n] = _o
    _o += _IN_OFF[_n][1]
P16_COLS = _o
LR_PAD = 256
LOG2E = 1.4426950408889634
_Q_SCALE = {"ga_q": GLA_HK ** -0.5, "wa_q": ATT_HD ** -0.5 * LOG2E}

VMEM_LIMIT = 56 * 1024 * 1024


def _nt(a, b):
    return lax.dot_general(a, b, (((1,), (1,)), ((), ())), preferred_element_type=F32)


def _tn(a, b):
    return lax.dot_general(a, b, (((0,), (0,)), ((), ())), preferred_element_type=F32)


def _log_sigmoid(z):
    return jnp.minimum(z, 0.0) - jnp.log(1.0 + jnp.exp(-jnp.abs(z)))


def _silu(x):
    return x * jax.nn.sigmoid(x)


def _norm_mod(x, g, m):
    y = x * lax.rsqrt(jnp.mean(x * x, axis=-1, keepdims=True) + EPS) * g
    return (y * (1.0 + m[:, D_MODEL:2 * D_MODEL]) + m[:, :D_MODEL]).astype(BF16)


def _mod_kernel(c_ref, w_ref, b_ref, o_ref):
    a = _silu(c_ref[...])
    o_ref[0] = jnp.dot(a.astype(BF16), w_ref[0].astype(BF16), preferred_element_type=F32) + b_ref[0]


def _modulation(cc, w_ada, b_ada):
    tn = 512
    return pl.pallas_call(
        _mod_kernel,
        grid=(DEPTH, 3 * D_MODEL // tn),
        in_specs=[
            pl.BlockSpec((16, D_MODEL), lambda l, j: (0, 0)),
            pl.BlockSpec((1, D_MODEL, tn), lambda l, j: (l, 0, j)),
            pl.BlockSpec((1, 1, tn), lambda l, j: (l, 0, j)),
        ],
        out_specs=pl.BlockSpec((1, 16, tn), lambda l, j: (l, 0, j)),
        out_shape=jax.ShapeDtypeStruct((DEPTH, 16, 3 * D_MODEL), F32),
        name="modulation",
    )(cc, w_ada, b_ada.reshape(DEPTH, 1, 3 * D_MODEL))


LAT_TILE = 512
CTX_TILE = 256
MERGE_ROWS = 256


def _lat_mod_row(t):
    return t // (SEQ // LAT_TILE)


def _ctx_mod_row(t):
    return BATCH


def _norm_kernel(x_ref, g_ref, mod_ref, h_ref):
    h_ref[...] = _norm_mod(x_ref[...], g_ref[...], mod_ref[0])


def _norm_modulate(x2d, norm_g, mod_l, tm, mod_row):
    rows = x2d.shape[0]
    return pl.pallas_call(
        _norm_kernel,
        grid=(rows // tm,),
        in_specs=[
            pl.BlockSpec((tm, D_MODEL), lambda t: (t, 0)),
            pl.BlockSpec((1, D_MODEL), lambda t: (0, 0)),
            pl.BlockSpec((1, 1, 3 * D_MODEL), lambda t: (mod_row(t), 0, 0)),
        ],
        out_specs=pl.BlockSpec((tm, D_MODEL), lambda t: (t, 0)),
        out_shape=jax.ShapeDtypeStruct((rows, D_MODEL), BF16),
        name="norm_modulate",
    )(x2d, norm_g.reshape(1, D_MODEL), mod_l)


PROJ_TILE = 2048
N_LAT_PROJ = R_LAT // PROJ_TILE
N_CTX_PROJ = R_CTX // PROJ_TILE


def _mm_kernel(hl_ref, hc_ref, w_ref, o_ref):
    i = pl.program_id(0)

    @pl.when(i < N_LAT_PROJ)
    def _():
        o_ref[...] = jnp.dot(hl_ref[...], w_ref[...], preferred_element_type=F32).astype(o_ref.dtype)

    @pl.when(i >= N_LAT_PROJ)
    def _():
        o_ref[...] = jnp.dot(hc_ref[...], w_ref[...], preferred_element_type=F32).astype(o_ref.dtype)


def _proj_specs(k, tn):
    tm = PROJ_TILE
    return [
        pl.BlockSpec((tm, k), lambda i, j: (jnp.minimum(i, N_LAT_PROJ - 1), 0)),
        pl.BlockSpec((tm, k), lambda i, j: (jnp.maximum(i - N_LAT_PROJ, 0), 0)),
        pl.BlockSpec((k, tn), lambda i, j: (0, j)),
    ]


def _in_proj(hl, hc, w, out_dtype, tn):
    k, n = w.shape
    return pl.pallas_call(
        _mm_kernel,
        grid=(N_LAT_PROJ + N_CTX_PROJ, n // tn),
        in_specs=_proj_specs(k, tn),
        out_specs=pl.BlockSpec((PROJ_TILE, tn), lambda i, j: (i, j)),
        out_shape=jax.ShapeDtypeStruct((ROWS, n), out_dtype),
        compiler_params=pltpu.CompilerParams(vmem_limit_bytes=VMEM_LIMIT),
        name="in_proj",
    )(hl, hc, w)


GATE_ROWS = 256


def _forget_proj_kernel(zero_bound, hl_ref, hc_ref, w_ref, lb_ref, o_ref):
    i = pl.program_id(0)
    a = lb_ref[0:1, :]
    l1m = lb_ref[1:2, :]

    def run(h_ref):
        for r in range(PROJ_TILE // GATE_ROWS):
            rows = slice(r * GATE_ROWS, (r + 1) * GATE_ROWS)
            log_f = _log_sigmoid(jnp.dot(h_ref[rows, :], w_ref[...], preferred_element_type=F32))
            if not zero_bound:
                c = l1m + log_f
                log_f = jnp.maximum(a, c) + jnp.log(1.0 + jnp.exp(-jnp.abs(a - c)))
            o_ref[rows, :] = log_f

    @pl.when(i < N_LAT_PROJ)
    def _():
        run(hl_ref)

    @pl.when(i >= N_LAT_PROJ)
    def _():
        run(hc_ref)


def _forget_proj(hl, hc, w, lb_logs, zero_bound):
    k, n = w.shape
    tn = 256
    return pl.pallas_call(
        functools.partial(_forget_proj_kernel, zero_bound),
        grid=(N_LAT_PROJ + N_CTX_PROJ, n // tn),
        in_specs=_proj_specs(k, tn) + [pl.BlockSpec((2, tn), lambda i, j: (0, j))],
        out_specs=pl.BlockSpec((PROJ_TILE, tn), lambda i, j: (i, j)),
        out_shape=jax.ShapeDtypeStruct((ROWS, n), F32),
        compiler_params=pltpu.CompilerParams(vmem_limit_bytes=VMEM_LIMIT),
        name="forget_proj",
    )(hl, hc, w, lb_logs)


CHUNK = 128
SUB = 16
N_SUB = CHUNK // SUB
STEP_CHUNKS = 2
STEP_ROWS = STEP_CHUNKS * CHUNK
LAT_CHUNKS = SEQ // STEP_ROWS
CTX_CHUNKS = CTX_LEN // STEP_ROWS
N_CHUNK = LAT_CHUNKS + CTX_CHUNKS
CTX_CHUNK0 = R_LAT // STEP_ROWS

_PAIR_LEVELS = tuple(SUB * 2 ** i for i in range(N_SUB.bit_length() - 1))
_GROUPS = tuple(2 ** (i + 1) for i in range(N_SUB.bit_length() - 1))
_FAC_ROWS = -(-8 * (2 * len(_GROUPS) + 1) // 16) * 16


def _fwd_rows(b, n):
    return jnp.where(n < CTX_CHUNKS, CTX_CHUNK0 + CTX_CHUNKS * b + n, LAT_CHUNKS * b + n - CTX_CHUNKS)


def _bwd_rows(b, n):
    return jnp.where(n < CTX_CHUNKS, CTX_CHUNK0 + CTX_CHUNKS * b + CTX_CHUNKS - 1 - n,
                     LAT_CHUNKS * b + N_CHUNK - 1 - n)


def _level_matrix(reverse):
    nb = N_SUB
    pos = [nb - 1 - m for m in range(nb)] if reverse else list(range(nb))
    mat = np.zeros((_FAC_ROWS, 16), np.float32)
    for gi, r in enumerate(_GROUPS):
        for m in range(nb):
            for m2 in range(nb):
                if m2 // r == m // r:
                    if pos[m2] < pos[m]:
                        mat[8 * gi + m, m2] = 1.0
                    if pos[m2] > pos[m]:
                        mat[8 * (len(_GROUPS) + gi) + m, m2] = 1.0
    mat[8 * 2 * len(_GROUPS), :nb] = 1.0
    return mat


def _cumsum_matrix(reverse):
    i = np.arange(CHUNK)[:, None]
    j = np.arange(CHUNK)[None, :]
    t = ((i // SUB == j // SUB) & ((j >= i) if reverse else (j <= i))).astype(np.float32)
    return np.concatenate([t, t], axis=1)


def _local_cumsum(x, reverse):
    n = x.shape[0]
    r8 = jnp.bitwise_and(lax.broadcasted_iota(jnp.int32, x.shape, 0), 7)
    for d in (1, 2, 4):
        if reverse:
            x = x + jnp.where(r8 < 8 - d, pltpu.roll(x, n - d, 0), 0.0)
        else:
            x = x + jnp.where(r8 >= d, pltpu.roll(x, d, 0), 0.0)
    out = []
    for m in range(n // SUB):
        lo = x[SUB * m:SUB * m + 8]
        hi = x[SUB * m + 8:SUB * m + SUB]
        if reverse:
            lo = lo + hi[0:1]
        else:
            hi = hi + lo[7:8]
        out += [lo, hi]
    return jnp.concatenate(out, axis=0)


def _rep_rows(x):
    return jnp.concatenate([jnp.broadcast_to(x[m:m + 1], (SUB, x.shape[1])) for m in range(N_SUB)], axis=0)


def _split3(x):
    hi = x.astype(BF16)
    r1 = x - hi.astype(F32)
    mid = r1.astype(BF16)
    lo = (r1 - mid.astype(F32)).astype(BF16)
    return hi, mid, lo


def _block_factors(bl, lmat, reverse):
    last = [SUB * m + (0 if reverse else SUB - 1) for m in range(N_SUB)]
    tot = jnp.concatenate([bl[i:i + 1] for i in last], axis=0)
    tot16 = jnp.concatenate([tot, jnp.zeros((16 - N_SUB, tot.shape[1]), F32)], axis=0)
    logs = sum(jnp.dot(lmat, t, preferred_element_type=F32) for t in _split3(tot16))
    return tot, jnp.exp(logs)


def _decay_variants(q, k, bl, tot, fac):
    qe = q * jnp.exp(bl)
    ke = k * jnp.exp(_rep_rows(tot) - bl)
    q_lv = {SUB: qe.astype(BF16)}
    k_lv = {SUB: ke.astype(BF16), "diag": (k * jnp.exp(-bl)).astype(BF16)}
    for gi in range(len(_GROUPS)):
        s = 2 * SUB * 2 ** gi
        q_lv[s] = (qe * _rep_rows(fac[8 * gi:8 * gi + N_SUB])).astype(BF16)
        fo = 8 * (len(_GROUPS) + gi)
        k_lv[s] = (ke * _rep_rows(fac[fo:fo + N_SUB])).astype(BF16)
    decay = fac[8 * 2 * len(_GROUPS):8 * 2 * len(_GROUPS) + 1]
    return q_lv, k_lv, decay


def _lead_rows(x, s, reverse):
    first = 0 if reverse else 1
    pieces = [x[(2 * g + first) * s:(2 * g + first + 1) * s] for g in range(CHUNK // (2 * s))]
    return pieces[0] if len(pieces) == 1 else jnp.concatenate(pieces, axis=0)


def _scan_masks(reverse):
    n = CHUNK
    i = lax.broadcasted_iota(jnp.int32, (n, n), 0)
    j = lax.broadcasted_iota(jnp.int32, (n, n), 1)
    sh = SUB.bit_length() - 1
    same = jnp.right_shift(i, sh) == jnp.right_shift(j, sh)
    masks = {"diag": jnp.logical_and(same, j >= i if reverse else j <= i)}
    rc = lax.broadcasted_iota(jnp.int32, (n // 2, n), 0)
    jc = lax.broadcasted_iota(jnp.int32, (n // 2, n), 1)
    for s in _PAIR_LEVELS:
        sh = s.bit_length() - 1
        masks[s] = jnp.right_shift(jc, sh) == 2 * jnp.right_shift(rc, sh) + (1 if reverse else 0)
    return masks


def _intra_scores(q_lv, k_lv, cols, masks, reverse):
    diag = jnp.where(masks["diag"], _nt(q_lv[SUB][:, cols], k_lv["diag"][:, cols]), 0.0)
    lead = {}
    for s in _PAIR_LEVELS:
        lead[s] = jnp.where(masks[s], _nt(_lead_rows(q_lv[s][:, cols], s, reverse), k_lv[s][:, cols]), 0.0)
    blocks = []
    for m in range(N_SUB):
        blk = diag[SUB * m:SUB * (m + 1)]
        for s in _PAIR_LEVELS:
            r = s // SUB
            ms = m // r
            if ms % 2 == (0 if reverse else 1):
                off = (ms // 2) * s + (m % r) * SUB
                blk = blk + lead[s][off:off + SUB]
        blocks.append(blk)
    return jnp.concatenate(blocks, axis=0).astype(BF16)


def _step_chunks(reverse):
    order = range(STEP_CHUNKS - 1, -1, -1) if reverse else range(STEP_CHUNKS)
    return [slice(c * CHUNK, (c + 1) * CHUNK) for c in order]


def _scan_step(directions, heads, hk, hv):
    chunks = [(d, rows) for d in directions for rows in _step_chunks(d[6])]
    bls = [d[0](rows) for d, rows in chunks]
    facs = [_block_factors(bl, d[2], d[6]) for (d, _), bl in zip(chunks, bls)]
    variants = [_decay_variants(*d[1](rows), bl, tot, fac) for (d, rows), bl, (tot, fac) in zip(chunks, bls, facs)]
    masks = {d[6]: _scan_masks(d[6]) for d in directions}
    heads_of = [(slice(h * hk, (h + 1) * hk), slice(h * hv, (h + 1) * hv)) for h in range(heads)]
    atts = [[_intra_scores(q_lv, k_lv, ks, masks[d[6]], d[6]) for ks, _ in heads_of]
            for (d, _), (q_lv, k_lv, _) in zip(chunks, variants)]
    intra = [[jnp.dot(a, d[3][rows, vs], preferred_element_type=F32) for a, (_, vs) in zip(att, heads_of)]
             for (d, rows), att in zip(chunks, atts)]
    for (d, rows), (q_lv, k_lv, decay), o_in in zip(chunks, variants, intra):
        _, _, _, v_ref, o_ref, st_ref, _ = d
        for h, (ks, vs) in enumerate(heads_of):
            st = st_ref[h]
            o_ref[rows, vs] = (o_in[h] + _nt(q_lv[CHUNK][:, ks], st.astype(BF16))).astype(o_ref.dtype)
            st_ref[h] = decay[:, ks] * st + _tn(v_ref[rows, vs], k_lv[CHUNK][:, ks])


def _reset_state(stf_ref, stb_ref):
    @pl.when(pl.program_id(1) == 0)
    def _():
        stf_ref[...] = jnp.zeros_like(stf_ref)
        stb_ref[...] = jnp.zeros_like(stb_ref)


def _gla_kernel(qf_ref, kf_ref, vf_ref, lrf_ref, qb_ref, kb_ref, vb_ref, lrb_ref, wz_ref, bz_ref,
                lmat_ref, of_ref, ob_ref, stf_ref, stb_ref):
    _reset_state(stf_ref, stb_ref)
    width = GLA_HEADS * GLA_HK
    gates = [jnp.dot(lr_ref[...].astype(BF16), wz_ref[:, d * width:(d + 1) * width], preferred_element_type=F32)
             + bz_ref[:, d * width:(d + 1) * width] for d, lr_ref in enumerate((lrf_ref, lrb_ref))]
    directions = []
    for d, (q_ref, k_ref, v_ref, o_ref, st_ref) in enumerate((
            (qf_ref, kf_ref, vf_ref, of_ref, stf_ref),
            (qb_ref, kb_ref, vb_ref, ob_ref, stb_ref))):
        reverse = d == 1

        def bl_of(rows, d=d, reverse=reverse):
            return _local_cumsum(_log_sigmoid(gates[d][rows]) / GLA_TAU, reverse)

        def qk_of(rows, q_ref=q_ref, k_ref=k_ref):
            return q_ref[rows, :].astype(F32), k_ref[rows, :].astype(F32)

        directions.append((bl_of, qk_of, lmat_ref[d], v_ref, o_ref, st_ref, reverse))
    _scan_step(directions, GLA_HEADS, GLA_HK, GLA_HV)


def _scan_call(kernel_fn, ins, args, name, heads, hk, hv, mxu_cumsum):
    out_sds = jax.ShapeDtypeStruct((ROWS, heads * hv), BF16)
    consts = []
    if mxu_cumsum:
        consts.append(np.stack([_cumsum_matrix(False), _cumsum_matrix(True)]))
    consts.append(np.stack([_level_matrix(False), _level_matrix(True)]))
    return pl.pallas_call(
        kernel_fn,
        grid=(BATCH, N_CHUNK),
        in_specs=ins + [pl.BlockSpec(c.shape, lambda b, n: (0, 0, 0)) for c in consts],
        out_specs=[pl.BlockSpec((STEP_ROWS, heads * hv), lambda b, n: (_fwd_rows(b, n), 0)),
                   pl.BlockSpec((STEP_ROWS, heads * hv), lambda b, n: (_bwd_rows(b, n), 0))],
        out_shape=[out_sds, out_sds],
        scratch_shapes=[pltpu.VMEM((heads, hv, hk), F32), pltpu.VMEM((heads, hv, hk), F32)],
        compiler_params=pltpu.CompilerParams(dimension_semantics=("arbitrary", "arbitrary"),
                                             vmem_limit_bytes=VMEM_LIMIT),
        name=name,
    )(*args, *[jnp.asarray(c, dtype=BF16) for c in consts])


def _chunk_spec(width, blk, rows_of):
    return pl.BlockSpec((STEP_ROWS, width), lambda b, n: (rows_of(b, n), blk))


def _gla(p16, p_lr, wz, bz):
    q_blk = _P16_OFF["ga_q"] // 512
    k_blk = _P16_OFF["ga_k"] // 512
    v_blk = _P16_OFF["ga_v"] // 1024
    ins, args = [], []
    for rows_of in (_fwd_rows, _bwd_rows):
        ins += [_chunk_spec(512, q_blk, rows_of), _chunk_spec(512, k_blk, rows_of),
                _chunk_spec(1024, v_blk, rows_of), _chunk_spec(LR_PAD, 0, rows_of)]
        args += [p16, p16, p16, p_lr]
    ins += [pl.BlockSpec((LR_PAD, 1024), lambda b, n: (0, 0)), pl.BlockSpec((1, 1024), lambda b, n: (0, 0))]
    args += [wz, bz]
    return _scan_call(_gla_kernel, ins, args, "gla_scan", GLA_HEADS, GLA_HK, GLA_HV, False)


def _hgrn_kernel(qf_ref, if_ref, ff_ref, qb_ref, ib_ref, fb_ref, cmat_ref, lmat_ref, of_ref, ob_ref,
                 stf_ref, stb_ref):
    _reset_state(stf_ref, stb_ref)
    directions = []
    for d, (q_ref, i_ref, f_ref, o_ref, st_ref) in enumerate((
            (qf_ref, if_ref, ff_ref, of_ref, stf_ref),
            (qb_ref, ib_ref, fb_ref, ob_ref, stb_ref))):

        def bl_of(rows, d=d, f_ref=f_ref):
            log_f = f_ref[rows, :]
            hi = log_f.astype(BF16)
            lo = (log_f - hi.astype(F32)).astype(BF16)
            return jnp.dot(cmat_ref[d], jnp.concatenate([hi, lo], axis=0), preferred_element_type=F32)

        def qk_of(rows, q_ref=q_ref, f_ref=f_ref):
            return q_ref[rows, :].astype(F32), 1.0 - jnp.exp(f_ref[rows, :])

        directions.append((bl_of, qk_of, lmat_ref[d], i_ref, o_ref, st_ref, d == 1))
    _scan_step(directions, HGRN_HEADS, HGRN_HK, HGRN_HV)


def _hgrn(p16, log_f):
    q_blk = _P16_OFF["hg_q"] // 1024
    i_blk = _P16_OFF["hg_i"] // 1024
    ins, args = [], []
    for d, rows_of in enumerate((_fwd_rows, _bwd_rows)):
        ins += [_chunk_spec(1024, blk, rows_of) for blk in (q_blk, i_blk, d)]
        args += [p16, p16, log_f]
    return _scan_call(_hgrn_kernel, ins, args, "hgrn_scan", HGRN_HEADS, HGRN_HK, HGRN_HV, True)


Q_STEP = 2 * ATT_BLOCK
LAT_BLKS = SEQ // ATT_BLOCK
LAT_STEPS = SEQ // Q_STEP
CTX_STEPS = CTX_LEN // Q_STEP
_FAR = 4 * ATT_BLOCK


def _rope(x, tab):
    lane = lax.broadcasted_iota(jnp.int32, x.shape, 1)
    partner = jnp.where(jnp.bitwise_and(lane, 32) == 0, pltpu.roll(x, 96, 1), pltpu.roll(x, 32, 1))
    return x * tab[:, :ATT_HD] + partner * tab[:, ATT_HD:]


def _window_mask(jblk, is_lat):
    blk = ATT_BLOCK
    off_prev = jnp.where(jnp.logical_and(is_lat, jblk >= 1), 0, _FAR)
    off_cur = jnp.where(is_lat, 0, _FAR)
    off_next = jnp.where(jnp.logical_and(is_lat, jblk + 1 < LAT_BLKS), 0, _FAR)
    qi = lax.broadcasted_iota(jnp.int32, (blk, 3 * blk), 0)
    col = lax.broadcasted_iota(jnp.int32, (blk, 3 * blk), 1)
    land, lor = jnp.logical_and, jnp.logical_or
    in_prev = land(col < blk, col >= qi + off_prev)
    in_cur = land(col >= blk + off_cur, col < 2 * blk)
    in_next = land(col >= 2 * blk, col - 2 * blk + off_next <= qi)
    return lor(in_prev, lor(in_cur, in_next))


def _attn_kernel(sink_ref, q_ref, g_ref, kvc_ref, kvm_ref, kvp_ref, kvn_ref, tm_ref, tp_ref, tn_ref, o_ref):
    step = pl.program_id(1)
    is_lat = step < LAT_STEPS
    blk = ATT_BLOCK
    kvw = ATT_HKV * ATT_HD
    tmid = tm_ref[...]
    tabs = [tp_ref[...], tmid[:blk], tmid[blk:], tn_ref[...]]
    masks = [_window_mask(2 * step + sub, is_lat) for sub in range(2)]
    units = []
    for kvh in range(ATT_HKV):
        kc = slice(kvh * ATT_HD, (kvh + 1) * ATT_HD)
        vc = slice(kvw + kvh * ATT_HD, kvw + (kvh + 1) * ATT_HD)
        heads = [kvh * ATT_GRP + g for g in range(ATT_GRP)]
        k_src = [kvp_ref[:, kc], kvm_ref[:blk, kc], kvm_ref[blk:, kc], kvn_ref[:, kc]]
        k_rot = [_rope(k.astype(F32), t).astype(BF16) for k, t in zip(k_src, tabs)]
        v_src = [kvp_ref[:, vc], kvm_ref[:blk, vc], kvm_ref[blk:, vc], kvn_ref[:, vc]]
        for sub in range(2):
            rows = slice(sub * blk, (sub + 1) * blk)
            qs = jnp.concatenate([_rope(q_ref[rows, h * ATT_HD:(h + 1) * ATT_HD].astype(F32), tabs[1 + sub])
                                  for h in heads], axis=0).astype(BF16)
            k_all = jnp.concatenate([kvc_ref[:, kc]] + k_rot[sub:sub + 3], axis=0)
            v_all = jnp.concatenate([kvc_ref[:, vc]] + v_src[sub:sub + 3], axis=0)
            units.append((heads, sub, rows, qs, k_all, v_all))
    scores = [_nt(qs, k_all) for _, _, _, qs, k_all, _ in units]
    probs = []
    for (heads, sub, _, _, _, _), s in zip(units, scores):
        ps, ls = [], []
        for g, h in enumerate(heads):
            s_ctx = s[g * blk:(g + 1) * blk, :CTX_LEN]
            s_loc = jnp.where(masks[sub], s[g * blk:(g + 1) * blk, CTX_LEN:], -jnp.inf)
            sk = sink_ref[h] * LOG2E
            m = jnp.maximum(jnp.maximum(jnp.max(s_ctx, axis=-1, keepdims=True),
                                        jnp.max(s_loc, axis=-1, keepdims=True)), sk)
            p = jnp.concatenate([jnp.exp2(s_ctx - m), jnp.exp2(s_loc - m)], axis=1)
            ls.append(jnp.sum(p, axis=-1, keepdims=True) + jnp.exp2(sk - m))
            ps.append(p.astype(BF16))
        probs.append((jnp.concatenate(ps, axis=0), ls))
    outs = [jnp.dot(p, v_all, preferred_element_type=F32) for (p, _), (_, _, _, _, _, v_all) in zip(probs, units)]
    for (heads, _, rows, _, _, _), (_, ls), out in zip(units, probs, outs):
        for g, h in enumerate(heads):
            hs = slice(h * ATT_HD, (h + 1) * ATT_HD)
            gate = _silu(g_ref[rows, hs].astype(F32))
            o_ref[rows, hs] = (out[g * blk:(g + 1) * blk, :] * (1.0 / ls[g]) * gate).astype(o_ref.dtype)


def _attention(p16, sink, rope_tab):
    blk = ATT_BLOCK
    qw = ATT_HQ * ATT_HD
    kvw = 2 * ATT_HKV * ATT_HD
    q_blk = _P16_OFF["wa_q"] // qw
    g_blk = _P16_OFF["wa_g"] // qw
    kv_blk = _P16_OFF["wa_k"] // kvw
    lat_q = R_LAT // Q_STEP

    def q_rows(b, s):
        return jnp.where(s < LAT_STEPS, LAT_STEPS * b + s, lat_q + CTX_STEPS * b + s - LAT_STEPS)

    def prev_blk(s):
        return jnp.clip(2 * s - 1, 0, LAT_BLKS - 1)

    def next_blk(s):
        return jnp.clip(2 * s + 2, 0, LAT_BLKS - 1)

    ins = [
        pl.BlockSpec(memory_space=pltpu.SMEM),
        pl.BlockSpec((Q_STEP, qw), lambda b, s: (q_rows(b, s), q_blk)),
        pl.BlockSpec((Q_STEP, qw), lambda b, s: (q_rows(b, s), g_blk)),
        pl.BlockSpec((CTX_LEN, kvw), lambda b, s: (R_LAT // CTX_LEN + b, kv_blk)),
        pl.BlockSpec((Q_STEP, kvw), lambda b, s: (q_rows(b, s), kv_blk)),
        pl.BlockSpec((blk, kvw), lambda b, s: (LAT_BLKS * b + prev_blk(s), kv_blk)),
        pl.BlockSpec((blk, kvw), lambda b, s: (LAT_BLKS * b + next_blk(s), kv_blk)),
        pl.BlockSpec((Q_STEP, 2 * ATT_HD), lambda b, s: (s, 0)),
        pl.BlockSpec((blk, 2 * ATT_HD), lambda b, s: (prev_blk(s), 0)),
        pl.BlockSpec((blk, 2 * ATT_HD), lambda b, s: (next_blk(s), 0)),
    ]
    return pl.pallas_call(
        _attn_kernel,
        grid=(BATCH, LAT_STEPS + CTX_STEPS),
        in_specs=ins,
        out_specs=pl.BlockSpec((Q_STEP, qw), lambda b, s: (q_rows(b, s), 0)),
        out_shape=jax.ShapeDtypeStruct((ROWS, qw), BF16),
        compiler_params=pltpu.CompilerParams(vmem_limit_bytes=VMEM_LIMIT),
        name="window_attn",
    )(sink, p16, p16, p16, p16, p16, p16, rope_tab, rope_tab, rope_tab)


def _head_norm(o, gain, heads, width):
    ys = []
    for h in range(heads):
        oh = o[:, h * width:(h + 1) * width]
        ys.append(oh * lax.rsqrt(jnp.mean(oh * oh, axis=-1, keepdims=True) + EPS) * gain)
    return jnp.concatenate(ys, axis=1)


def _merge_kernel(final, ogf_ref, ogb_ref, gg_ref, ohf_ref, ohb_ref, hg_ref, ya_ref, mg_ref, x_ref,
                  mod_ref, gn_gla_ref, gn_hg_ref, wbr_ref, wout_ref, *rest):
    for r in range(x_ref.shape[0] // MERGE_ROWS):
        rows = slice(r * MERGE_ROWS, (r + 1) * MERGE_ROWS)
        o_gla = ogf_ref[rows, :].astype(F32) + ogb_ref[rows, :].astype(F32)
        o_hg = ohf_ref[rows, :].astype(F32) + ohb_ref[rows, :].astype(F32)
        y_gla = _head_norm(o_gla, gn_gla_ref[...], GLA_HEADS, GLA_HV) * _silu(gg_ref[rows, :].astype(F32))
        y_hg = _head_norm(o_hg, gn_hg_ref[...], HGRN_HEADS, HGRN_HV) * _silu(hg_ref[rows, :].astype(F32))
        ys = (y_gla.astype(BF16), y_hg.astype(BF16), ya_ref[rows, :])
        merged = None
        for n in range(N_BRANCH):
            proj = jnp.dot(ys[n], wbr_ref[n], preferred_element_type=F32)
            term = jax.nn.sigmoid(mg_ref[rows, n * D_MODEL:(n + 1) * D_MODEL].astype(F32)) * proj
            merged = term if merged is None else merged + term
        upd = jnp.dot(merged.astype(BF16), wout_ref[...], preferred_element_type=F32)
        x_new = x_ref[rows, :] + mod_ref[0][:, 2 * D_MODEL:] * upd
        if final:
            fg_ref, out_ref = rest
            out_ref[rows, :] = (x_new * lax.rsqrt(jnp.mean(x_new * x_new, axis=-1, keepdims=True) + EPS)
                                * fg_ref[...])
        else:
            ng_ref, nmod_ref, xo_ref, h_ref = rest
            xo_ref[rows, :] = x_new
            h_ref[rows, :] = _norm_mod(x_new, ng_ref[...], nmod_ref[0])


def _merge(final, context, og, oh, y_att, p16, x2d, mod_l, gn_gla, gn_hg, wbr, wout, tail):
    tm = CTX_TILE if context else LAT_TILE
    rows = x2d.shape[0]
    blk0 = R_LAT // tm if context else 0
    mod_row = _ctx_mod_row if context else _lat_mod_row

    def shared(width, col=0):
        return pl.BlockSpec((tm, width), lambda t: (blk0 + t, col))

    def whole(shape):
        return pl.BlockSpec(shape, lambda t: (0,) * len(shape), pipeline_mode=pl.Buffered(1))

    ins = [
        shared(1024), shared(1024), shared(1024, _P16_OFF["ga_g"] // 1024),
        shared(1024), shared(1024), shared(1024, _P16_OFF["hg_g"] // 1024),
        shared(1024), shared(3072, _P16_OFF["mg"] // 3072),
        pl.BlockSpec((tm, D_MODEL), lambda t: (t, 0)),
        pl.BlockSpec((1, 1, 3 * D_MODEL), lambda t: (mod_row(t), 0, 0)),
        whole((1, GLA_HV)), whole((1, HGRN_HV)),
        whole((N_BRANCH, D_MODEL, D_MODEL)), whole((D_MODEL, D_MODEL)),
        whole((1, D_MODEL)),
    ]
    args = [og[0], og[1], p16, oh[0], oh[1], p16, y_att, p16, x2d, mod_l,
            gn_gla.reshape(1, GLA_HV), gn_hg.reshape(1, HGRN_HV), wbr, wout, tail[0].reshape(1, D_MODEL)]
    row_spec = pl.BlockSpec((tm, D_MODEL), lambda t: (t, 0))
    if final:
        out_specs = row_spec
        out_shape = jax.ShapeDtypeStruct((rows, D_MODEL), F32)
    else:
        ins.append(pl.BlockSpec((1, 1, 3 * D_MODEL), lambda t: (mod_row(t), 0, 0)))
        args.append(tail[1])
        out_specs = [row_spec, row_spec]
        out_shape = [jax.ShapeDtypeStruct((rows, D_MODEL), F32), jax.ShapeDtypeStruct((rows, D_MODEL), BF16)]
    return pl.pallas_call(
        functools.partial(_merge_kernel, final),
        grid=(rows // tm,),
        in_specs=ins,
        out_specs=out_specs,
        out_shape=out_shape,
        compiler_params=pltpu.CompilerParams(vmem_limit_bytes=VMEM_LIMIT),
        name="merge_final" if final else ("merge_ctx" if context else "merge"),
    )(*args)


def _rope_table():
    r = ATT_HD // 4
    inv = ROPE_BASE ** (-jnp.arange(r, dtype=F32) / r)
    t = jnp.arange(SEQ)
    ang_row = (t // GRID_W).astype(F32)[:, None] * inv
    ang_col = (t % GRID_W).astype(F32)[:, None] * inv
    cos = jnp.concatenate([jnp.cos(ang_row)] * 2 + [jnp.cos(ang_col)] * 2, axis=1)
    sin = jnp.concatenate([-jnp.sin(ang_row), jnp.sin(ang_row), -jnp.sin(ang_col), jnp.sin(ang_col)], axis=1)
    tab = jnp.concatenate([cos, sin], axis=1)
    ident = jnp.concatenate([jnp.ones((CTX_LEN, ATT_HD), F32), jnp.zeros((CTX_LEN, ATT_HD), F32)], axis=1)
    return jnp.concatenate([tab, ident], axis=0)


def _cols(w, name):
    o, s = _IN_OFF[name]
    return w[:, o:o + s] * _Q_SCALE.get(name, 1.0)


def kernel(x, c, ctx, c_ctx, norm_g, w_ada, b_ada, w_in, gla_w_a2, gla_b_a2, gla_norm_g, hgrn_lb_logits,
           hgrn_norm_g, attn_sink, w_branch, w_out, final_g):
    cc = jnp.zeros((16, D_MODEL), F32).at[:BATCH].set(c).at[BATCH].set(c_ctx)
    mod = _modulation(cc, w_ada, b_ada).reshape(DEPTH, 16, 1, 3 * D_MODEL)
    rope_tab = _rope_table()
    lb_cum = jnp.cumsum(jax.nn.softmax(hgrn_lb_logits.astype(F32), axis=0), axis=0)
    lower_bounds = lb_cum - lb_cum[0]

    xl = x.reshape(R_LAT, D_MODEL)
    xc = ctx.reshape(R_CTX, D_MODEL)
    hl = _norm_modulate(xl, norm_g[0], mod[0], LAT_TILE, _lat_mod_row)
    hc = _norm_modulate(xc, norm_g[0], mod[0], CTX_TILE, _ctx_mod_row)
    out = None
    for l in range(DEPTH):
        final = l == DEPTH - 1
        w = w_in[l]
        w16 = jnp.concatenate([_cols(w, n) for n in _P16_ORDER], axis=1).astype(BF16)
        w_f = _cols(w, "hg_f").astype(BF16)
        w_lr = jnp.concatenate([_cols(w, "ga_lr"), jnp.zeros((D_MODEL, LR_PAD - 2 * GLA_RANK), F32)],
                               axis=1).astype(BF16)
        wz = jnp.zeros((LR_PAD, 2 * GLA_HEADS * GLA_HK), F32)
        wz = wz.at[:GLA_RANK, :GLA_HEADS * GLA_HK].set(gla_w_a2[l, 0])
        wz = wz.at[GLA_RANK:2 * GLA_RANK, GLA_HEADS * GLA_HK:].set(gla_w_a2[l, 1]).astype(BF16)
        bz = gla_b_a2[l].reshape(1, 2 * GLA_HEADS * GLA_HK)
        lb = lower_bounds[l]
        lb_logs = jnp.stack([jnp.log(lb), jnp.log1p(-lb)], axis=0).reshape(2, 2 * HGRN_F)

        p16 = _in_proj(hl, hc, w16, BF16, 512)
        p_lr = _in_proj(hl, hc, w_lr, F32, LR_PAD)
        log_f = _forget_proj(hl, hc, w_f, lb_logs, l == 0)
        og = _gla(p16, p_lr, wz, bz)
        oh = _hgrn(p16, log_f)
        y_att = _attention(p16, attn_sink[l], rope_tab)
        common = (og, oh, y_att, p16)
        params = (mod[l], gla_norm_g[l], hgrn_norm_g[l], w_branch[l].astype(BF16), w_out[l].astype(BF16))
        if final:
            out = _merge(True, False, *common, xl, *params, (final_g,))
        else:
            tail = (norm_g[l + 1], mod[l + 1])
            xl, hl = _merge(False, False, *common, xl, *params, tail)
            xc, hc = _merge(False, True, *common, xc, *params, tail)
    return out.reshape(BATCH, SEQ, D_MODEL)
```

```python
import functools

import numpy as np
import jax
import jax.numpy as jnp
from jax import lax
from jax.experimental import pallas as pl
from jax.experimental.pallas import tpu as pltpu

F32 = jnp.float32
BF16 = jnp.bfloat16

D_MODEL = 1024
BATCH = 8
SEQ = 4096
DEPTH = 2
CTX_LEN = 256
GRID_W = 64
N_BRANCH = 3
GLA_HEADS = 4
GLA_HV = 256
GLA_HK = 128
GLA_RANK = 16
GLA_TAU = 16.0
HGRN_HEADS = 8
HGRN_HV = 128
HGRN_HK = 128
HGRN_F = HGRN_HEADS * HGRN_HK
ATT_HD = 128
ATT_HQ = 8
ATT_HKV = 2
ATT_GRP = ATT_HQ // ATT_HKV
WINDOW = 128
ATT_BLOCK = 128
ROPE_BASE = 10000.0
EPS = 1e-6

R_LAT = BATCH * SEQ
R_CTX = BATCH * CTX_LEN
ROWS = R_LAT + R_CTX

_IN_SIZES = (512, 512, 1024, 1024, 32, 1024, 2048, 1024, 1024, 1024, 256, 256, 1024, 3072)
_IN_NAMES = ("ga_q", "ga_k", "ga_v", "ga_g", "ga_lr", "hg_q", "hg_f", "hg_i", "hg_g",
             "wa_q", "wa_k", "wa_v", "wa_g", "mg")
_IN_OFF = {}
_o = 0
for _n, _s in zip(_IN_NAMES, _IN_SIZES):
    _IN_OFF[_n] = (_o, _s)
    _o += _s

_P16_ORDER = ("mg", "ga_v", "ga_g", "hg_q", "hg_i", "hg_g", "wa_q", "wa_g", "ga_q", "ga_k", "wa_k", "wa_v")
_P16_OFF = {}
_o = 0
for _n in _P16_ORDER:
    _P16_OFF[_n] = _o
    _o += _IN_OFF[_n][1]
P16_COLS = _o
LR_PAD = 256
LOG2E = 1.4426950408889634
_Q_SCALE = {"ga_q": GLA_HK ** -0.5, "wa_q": ATT_HD ** -0.5 * LOG2E}

VMEM_LIMIT = 56 * 1024 * 1024


def _nt(a, b):
    return lax.dot_general(a, b, (((1,), (1,)), ((), ())), preferred_element_type=F32)


def _tn(a, b):
    return lax.dot_general(a, b, (((0,), (0,)), ((), ())), preferred_element_type=F32)


def _log_sigmoid(z):
    return jnp.minimum(z, 0.0) - jnp.log(1.0 + jnp.exp(-jnp.abs(z)))


def _silu(x):
    return x * jax.nn.sigmoid(x)


def _norm_mod(x, g, m):
    y = x * lax.rsqrt(jnp.mean(x * x, axis=-1, keepdims=True) + EPS) * g
    return (y * (1.0 + m[:, D_MODEL:2 * D_MODEL]) + m[:, :D_MODEL]).astype(BF16)


def _mod_kernel(c_ref, w_ref, b_ref, o_ref):
    a = _silu(c_ref[...])
    o_ref[0] = jnp.dot(a.astype(BF16), w_ref[0].astype(BF16), preferred_element_type=F32) + b_ref[0]


def _modulation(cc, w_ada, b_ada):
    tn = 512
    return pl.pallas_call(
        _mod_kernel,
        grid=(DEPTH, 3 * D_MODEL // tn),
        in_specs=[
            pl.BlockSpec((16, D_MODEL), lambda l, j: (0, 0)),
            pl.BlockSpec((1, D_MODEL, tn), lambda l, j: (l, 0, j)),
            pl.BlockSpec((1, 1, tn), lambda l, j: (l, 0, j)),
        ],
        out_specs=pl.BlockSpec((1, 16, tn), lambda l, j: (l, 0, j)),
        out_shape=jax.ShapeDtypeStruct((DEPTH, 16, 3 * D_MODEL), F32),
        name="modulation",
    )(cc, w_ada, b_ada.reshape(DEPTH, 1, 3 * D_MODEL))


LAT_TILE = 512
CTX_TILE = 256
MERGE_ROWS = 256


def _lat_mod_row(t):
    return t // (SEQ // LAT_TILE)


def _ctx_mod_row(t):
    return BATCH


def _norm_kernel(x_ref, g_ref, mod_ref, h_ref):
    h_ref[...] = _norm_mod(x_ref[...], g_ref[...], mod_ref[0])


def _norm_modulate(x2d, norm_g, mod_l, tm, mod_row):
    rows = x2d.shape[0]
    return pl.pallas_call(
        _norm_kernel,
        grid=(rows // tm,),
        in_specs=[
            pl.BlockSpec((tm, D_MODEL), lambda t: (t, 0)),
            pl.BlockSpec((1, D_MODEL), lambda t: (0, 0)),
            pl.BlockSpec((1, 1, 3 * D_MODEL), lambda t: (mod_row(t), 0, 0)),
        ],
        out_specs=pl.BlockSpec((tm, D_MODEL), lambda t: (t, 0)),
        out_shape=jax.ShapeDtypeStruct((rows, D_MODEL), BF16),
        name="norm_modulate",
    )(x2d, norm_g.reshape(1, D_MODEL), mod_l)


PROJ_TILE = 2048
N_LAT_PROJ = R_LAT // PROJ_TILE
N_CTX_PROJ = R_CTX // PROJ_TILE


def _mm_kernel(hl_ref, hc_ref, w_ref, o_ref):
    i = pl.program_id(0)

    @pl.when(i < N_LAT_PROJ)
    def _():
        o_ref[...] = jnp.dot(hl_ref[...], w_ref[...], preferred_element_type=F32).astype(o_ref.dtype)

    @pl.when(i >= N_LAT_PROJ)
    def _():
        o_ref[...] = jnp.dot(hc_ref[...], w_ref[...], preferred_element_type=F32).astype(o_ref.dtype)


def _proj_specs(k, tn):
    tm = PROJ_TILE
    return [
        pl.BlockSpec((tm, k), lambda i, j: (jnp.minimum(i, N_LAT_PROJ - 1), 0)),
        pl.BlockSpec((tm, k), lambda i, j: (jnp.maximum(i - N_LAT_PROJ, 0), 0)),
        pl.BlockSpec((k, tn), lambda i, j: (0, j)),
    ]


def _in_proj(hl, hc, w, out_dtype, tn):
    k, n = w.shape
    return pl.pallas_call(
        _mm_kernel,
        grid=(N_LAT_PROJ + N_CTX_PROJ, n // tn),
        in_specs=_proj_specs(k, tn),
        out_specs=pl.BlockSpec((PROJ_TILE, tn), lambda i, j: (i, j)),
        out_shape=jax.ShapeDtypeStruct((ROWS, n), out_dtype),
        compiler_params=pltpu.CompilerParams(vmem_limit_bytes=VMEM_LIMIT),
        name="in_proj",
    )(hl, hc, w)


GATE_ROWS = 256


def _forget_proj_kernel(zero_bound, hl_ref, hc_ref, w_ref, lb_ref, o_ref):
    i = pl.program_id(0)
    a = lb_ref[0:1, :]
    l1m = lb_ref[1:2, :]

    def run(h_ref):
        for r in range(PROJ_TILE // GATE_ROWS):
            rows = slice(r * GATE_ROWS, (r + 1) * GATE_ROWS)
            log_f = _log_sigmoid(jnp.dot(h_ref[rows, :], w_ref[...], preferred_element_type=F32))
            if not zero_bound:
                c = l1m + log_f
                log_f = jnp.maximum(a, c) + jnp.log(1.0 + jnp.exp(-jnp.abs(a - c)))
            o_ref[rows, :] = log_f

    @pl.when(i < N_LAT_PROJ)
    def _():
        run(hl_ref)

    @pl.when(i >= N_LAT_PROJ)
    def _():
        run(hc_ref)


def _forget_proj(hl, hc, w, lb_logs, zero_bound):
    k, n = w.shape
    tn = 256
    return pl.pallas_call(
        functools.partial(_forget_proj_kernel, zero_bound),
        grid=(N_LAT_PROJ + N_CTX_PROJ, n // tn),
        in_specs=_proj_specs(k, tn) + [pl.BlockSpec((2, tn), lambda i, j: (0, j))],
        out_specs=pl.BlockSpec((PROJ_TILE, tn), lambda i, j: (i, j)),
        out_shape=jax.ShapeDtypeStruct((ROWS, n), F32),
        compiler_params=pltpu.CompilerParams(vmem_limit_bytes=VMEM_LIMIT),
        name="forget_proj",
    )(hl, hc, w, lb_logs)


CHUNK = 128
SUB = 16
N_SUB = CHUNK // SUB
STEP_CHUNKS = 2
STEP_ROWS = STEP_CHUNKS * CHUNK
LAT_CHUNKS = SEQ // STEP_ROWS
CTX_CHUNKS = CTX_LEN // STEP_ROWS
N_CHUNK = LAT_CHUNKS + CTX_CHUNKS
CTX_CHUNK0 = R_LAT // STEP_ROWS

_PAIR_LEVELS = tuple(SUB * 2 ** i for i in range(N_SUB.bit_length() - 1))
_GROUPS = tuple(2 ** (i + 1) for i in range(N_SUB.bit_length() - 1))
_FAC_ROWS = -(-8 * (2 * len(_GROUPS) + 1) // 16) * 16


def _fwd_rows(b, n):
    return jnp.where(n < CTX_CHUNKS, CTX_CHUNK0 + CTX_CHUNKS * b + n, LAT_CHUNKS * b + n - CTX_CHUNKS)


def _bwd_rows(b, n):
    return jnp.where(n < CTX_CHUNKS, CTX_CHUNK0 + CTX_CHUNKS * b + CTX_CHUNKS - 1 - n,
                     LAT_CHUNKS * b + N_CHUNK - 1 - n)


def _level_matrix(reverse):
    nb = N_SUB
    pos = [nb - 1 - m for m in range(nb)] if reverse else list(range(nb))
    mat = np.zeros((_FAC_ROWS, 16), np.float32)
    for gi, r in enumerate(_GROUPS):
        for m in range(nb):
            for m2 in range(nb):
                if m2 // r == m // r:
                    if pos[m2] < pos[m]:
                        mat[8 * gi + m, m2] = 1.0
                    if pos[m2] > pos[m]:
                        mat[8 * (len(_GROUPS) + gi) + m, m2] = 1.0
    mat[8 * 2 * len(_GROUPS), :nb] = 1.0
    return mat


def _cumsum_matrix(reverse):
    i = np.arange(CHUNK)[:, None]
    j = np.arange(CHUNK)[None, :]
    t = ((i // SUB == j // SUB) & ((j >= i) if reverse else (j <= i))).astype(np.float32)
    return np.concatenate([t, t], axis=1)


def _mxu_cumsum(la, cmat):
    hi = la.astype(BF16)
    lo = (la - hi.astype(F32)).astype(BF16)
    return jnp.dot(cmat, jnp.concatenate([hi, lo], axis=0), preferred_element_type=F32)


def _rep_rows(x):
    return jnp.concatenate([jnp.broadcast_to(x[m:m + 1], (SUB, x.shape[1])) for m in range(N_SUB)], axis=0)


def _split3(x):
    hi = x.astype(BF16)
    r1 = x - hi.astype(F32)
    mid = r1.astype(BF16)
    lo = (r1 - mid.astype(F32)).astype(BF16)
    return hi, mid, lo


def _block_factors(bl, lmat, reverse):
    last = [SUB * m + (0 if reverse else SUB - 1) for m in range(N_SUB)]
    tot = jnp.concatenate([bl[i:i + 1] for i in last], axis=0)
    tot16 = jnp.concatenate([tot, jnp.zeros((16 - N_SUB, tot.shape[1]), F32)], axis=0)
    logs = sum(jnp.dot(lmat, t, preferred_element_type=F32) for t in _split3(tot16))
    return tot, jnp.exp(logs)


def _decay_variants(q, k, bl, tot, fac):
    qe = q * jnp.exp(bl)
    ke = k * jnp.exp(_rep_rows(tot) - bl)
    q_lv = {SUB: qe.astype(BF16)}
    k_lv = {SUB: ke.astype(BF16), "diag": (k * jnp.exp(-bl)).astype(BF16)}
    for gi in range(len(_GROUPS)):
        s = 2 * SUB * 2 ** gi
        q_lv[s] = (qe * _rep_rows(fac[8 * gi:8 * gi + N_SUB])).astype(BF16)
        fo = 8 * (len(_GROUPS) + gi)
        k_lv[s] = (ke * _rep_rows(fac[fo:fo + N_SUB])).astype(BF16)
    decay = fac[8 * 2 * len(_GROUPS):8 * 2 * len(_GROUPS) + 1]
    return q_lv, k_lv, decay


def _lead_rows(x, s, reverse):
    first = 0 if reverse else 1
    pieces = [x[(2 * g + first) * s:(2 * g + first + 1) * s] for g in range(CHUNK // (2 * s))]
    return pieces[0] if len(pieces) == 1 else jnp.concatenate(pieces, axis=0)


def _scan_masks(reverse):
    n = CHUNK
    i = lax.broadcasted_iota(jnp.int32, (n, n), 0)
    j = lax.broadcasted_iota(jnp.int32, (n, n), 1)
    sh = SUB.bit_length() - 1
    same = jnp.right_shift(i, sh) == jnp.right_shift(j, sh)
    masks = {"diag": jnp.logical_and(same, j >= i if reverse else j <= i)}
    rc = lax.broadcasted_iota(jnp.int32, (n // 2, n), 0)
    jc = lax.broadcasted_iota(jnp.int32, (n // 2, n), 1)
    for s in _PAIR_LEVELS:
        sh = s.bit_length() - 1
        masks[s] = jnp.right_shift(jc, sh) == 2 * jnp.right_shift(rc, sh) + (1 if reverse else 0)
    return masks


def _intra_scores(q_lv, k_lv, cols, masks, reverse):
    diag = jnp.where(masks["diag"], _nt(q_lv[SUB][:, cols], k_lv["diag"][:, cols]), 0.0)
    lead = {}
    for s in _PAIR_LEVELS:
        lead[s] = jnp.where(masks[s], _nt(_lead_rows(q_lv[s][:, cols], s, reverse), k_lv[s][:, cols]), 0.0)
    blocks = []
    for m in range(N_SUB):
        blk = diag[SUB * m:SUB * (m + 1)]
        for s in _PAIR_LEVELS:
            r = s // SUB
            ms = m // r
            if ms % 2 == (0 if reverse else 1):
                off = (ms // 2) * s + (m % r) * SUB
                blk = blk + lead[s][off:off + SUB]
        blocks.append(blk)
    return jnp.concatenate(blocks, axis=0).astype(BF16)


def _step_chunks(reverse):
    order = range(STEP_CHUNKS - 1, -1, -1) if reverse else range(STEP_CHUNKS)
    return [slice(c * CHUNK, (c + 1) * CHUNK) for c in order]


def _scan_step(directions, heads, hk, hv):
    chunks = [(d, rows) for d in directions for rows in _step_chunks(d[6])]
    bls = [d[0](rows) for d, rows in chunks]
    facs = [_block_factors(bl, d[2], d[6]) for (d, _), bl in zip(chunks, bls)]
    variants = [_decay_variants(*d[1](rows), bl, tot, fac) for (d, rows), bl, (tot, fac) in zip(chunks, bls, facs)]
    masks = {d[6]: _scan_masks(d[6]) for d in directions}
    heads_of = [(slice(h * hk, (h + 1) * hk), slice(h * hv, (h + 1) * hv)) for h in range(heads)]
    atts = [[_intra_scores(q_lv, k_lv, ks, masks[d[6]], d[6]) for ks, _ in heads_of]
            for (d, _), (q_lv, k_lv, _) in zip(chunks, variants)]
    intra = [[jnp.dot(a, d[3][rows, vs], preferred_element_type=F32) for a, (_, vs) in zip(att, heads_of)]
             for (d, rows), att in zip(chunks, atts)]
    for (d, rows), (q_lv, k_lv, decay), o_in in zip(chunks, variants, intra):
        _, _, _, v_ref, o_ref, st_ref, _ = d
        for h, (ks, vs) in enumerate(heads_of):
            st = st_ref[h]
            o_ref[rows, vs] = (o_in[h] + _nt(q_lv[CHUNK][:, ks], st.astype(BF16))).astype(o_ref.dtype)
            st_ref[h] = decay[:, ks] * st + _tn(v_ref[rows, vs], k_lv[CHUNK][:, ks])


def _reset_state(stf_ref, stb_ref):
    @pl.when(pl.program_id(1) == 0)
    def _():
        stf_ref[...] = jnp.zeros_like(stf_ref)
        stb_ref[...] = jnp.zeros_like(stb_ref)


def _gla_kernel(qf_ref, kf_ref, vf_ref, lrf_ref, qb_ref, kb_ref, vb_ref, lrb_ref, wz_ref, bz_ref,
                cmat_ref, lmat_ref, of_ref, ob_ref, stf_ref, stb_ref):
    _reset_state(stf_ref, stb_ref)
    width = GLA_HEADS * GLA_HK
    gates = [jnp.dot(lr_ref[...].astype(BF16), wz_ref[:, d * width:(d + 1) * width], preferred_element_type=F32)
             + bz_ref[:, d * width:(d + 1) * width] for d, lr_ref in enumerate((lrf_ref, lrb_ref))]
    directions = []
    for d, (q_ref, k_ref, v_ref, o_ref, st_ref) in enumerate((
            (qf_ref, kf_ref, vf_ref, of_ref, stf_ref),
            (qb_ref, kb_ref, vb_ref, ob_ref, stb_ref))):
        reverse = d == 1

        def bl_of(rows, d=d):
            return _mxu_cumsum(_log_sigmoid(gates[d][rows]) / GLA_TAU, cmat_ref[d])

        def qk_of(rows, q_ref=q_ref, k_ref=k_ref):
            return q_ref[rows, :].astype(F32), k_ref[rows, :].astype(F32)

        directions.append((bl_of, qk_of, lmat_ref[d], v_ref, o_ref, st_ref, reverse))
    _scan_step(directions, GLA_HEADS, GLA_HK, GLA_HV)


def _scan_call(kernel_fn, ins, args, name, heads, hk, hv):
    out_sds = jax.ShapeDtypeStruct((ROWS, heads * hv), BF16)
    consts = [np.stack([_cumsum_matrix(False), _cumsum_matrix(True)]),
              np.stack([_level_matrix(False), _level_matrix(True)])]
    return pl.pallas_call(
        kernel_fn,
        grid=(BATCH, N_CHUNK),
        in_specs=ins + [pl.BlockSpec(c.shape, lambda b, n: (0, 0, 0)) for c in consts],
        out_specs=[pl.BlockSpec((STEP_ROWS, heads * hv), lambda b, n: (_fwd_rows(b, n), 0)),
                   pl.BlockSpec((STEP_ROWS, heads * hv), lambda b, n: (_bwd_rows(b, n), 0))],
        out_shape=[out_sds, out_sds],
        scratch_shapes=[pltpu.VMEM((heads, hv, hk), F32), pltpu.VMEM((heads, hv, hk), F32)],
        compiler_params=pltpu.CompilerParams(dimension_semantics=("arbitrary", "arbitrary"),
                                             vmem_limit_bytes=VMEM_LIMIT),
        name=name,
    )(*args, *[jnp.asarray(c, dtype=BF16) for c in consts])


def _chunk_spec(width, blk, rows_of):
    return pl.BlockSpec((STEP_ROWS, width), lambda b, n: (rows_of(b, n), blk))


def _gla(p16, p_lr, wz, bz):
    q_blk = _P16_OFF["ga_q"] // 512
    k_blk = _P16_OFF["ga_k"] // 512
    v_blk = _P16_OFF["ga_v"] // 1024
    ins, args = [], []
    for rows_of in (_fwd_rows, _bwd_rows):
        ins += [_chunk_spec(512, q_blk, rows_of), _chunk_spec(512, k_blk, rows_of),
                _chunk_spec(1024, v_blk, rows_of), _chunk_spec(LR_PAD, 0, rows_of)]
        args += [p16, p16, p16, p_lr]
    ins += [pl.BlockSpec((LR_PAD, 1024), lambda b, n: (0, 0)), pl.BlockSpec((1, 1024), lambda b, n: (0, 0))]
    args += [wz, bz]
    return _scan_call(_gla_kernel, ins, args, "gla_scan", GLA_HEADS, GLA_HK, GLA_HV)


def _hgrn_kernel(qf_ref, if_ref, ff_ref, qb_ref, ib_ref, fb_ref, cmat_ref, lmat_ref, of_ref, ob_ref,
                 stf_ref, stb_ref):
    _reset_state(stf_ref, stb_ref)
    directions = []
    for d, (q_ref, i_ref, f_ref, o_ref, st_ref) in enumerate((
            (qf_ref, if_ref, ff_ref, of_ref, stf_ref),
            (qb_ref, ib_ref, fb_ref, ob_ref, stb_ref))):

        def bl_of(rows, d=d, f_ref=f_ref):
            return _mxu_cumsum(f_ref[rows, :], cmat_ref[d])

        def qk_of(rows, q_ref=q_ref, f_ref=f_ref):
            return q_ref[rows, :].astype(F32), 1.0 - jnp.exp(f_ref[rows, :])

        directions.append((bl_of, qk_of, lmat_ref[d], i_ref, o_ref, st_ref, d == 1))
    _scan_step(directions, HGRN_HEADS, HGRN_HK, HGRN_HV)


def _hgrn(p16, log_f):
    q_blk = _P16_OFF["hg_q"] // 1024
    i_blk = _P16_OFF["hg_i"] // 1024
    ins, args = [], []
    for d, rows_of in enumerate((_fwd_rows, _bwd_rows)):
        ins += [_chunk_spec(1024, blk, rows_of) for blk in (q_blk, i_blk, d)]
        args += [p16, p16, log_f]
    return _scan_call(_hgrn_kernel, ins, args, "hgrn_scan", HGRN_HEADS, HGRN_HK, HGRN_HV)


Q_STEP = 2 * ATT_BLOCK
LAT_BLKS = SEQ // ATT_BLOCK
LAT_STEPS = SEQ // Q_STEP
CTX_STEPS = CTX_LEN // Q_STEP
_FAR = 4 * ATT_BLOCK


def _rope(x, tab):
    lane = lax.broadcasted_iota(jnp.int32, x.shape, 1)
    partner = jnp.where(jnp.bitwise_and(lane, 32) == 0, pltpu.roll(x, 96, 1), pltpu.roll(x, 32, 1))
    return x * tab[:, :ATT_HD] + partner * tab[:, ATT_HD:]


def _window_mask(jblk, is_lat):
    blk = ATT_BLOCK
    off_prev = jnp.where(jnp.logical_and(is_lat, jblk >= 1), 0, _FAR)
    off_cur = jnp.where(is_lat, 0, _FAR)
    off_next = jnp.where(jnp.logical_and(is_lat, jblk + 1 < LAT_BLKS), 0, _FAR)
    qi = lax.broadcasted_iota(jnp.int32, (blk, 3 * blk), 0)
    col = lax.broadcasted_iota(jnp.int32, (blk, 3 * blk), 1)
    land, lor = jnp.logical_and, jnp.logical_or
    in_prev = land(col < blk, col >= qi + off_prev)
    in_cur = land(col >= blk + off_cur, col < 2 * blk)
    in_next = land(col >= 2 * blk, col - 2 * blk + off_next <= qi)
    return lor(in_prev, lor(in_cur, in_next))


def _attn_kernel(sink_ref, q_ref, g_ref, kvc_ref, kvm_ref, kvp_ref, kvn_ref, tm_ref, tp_ref, tn_ref, o_ref):
    step = pl.program_id(1)
    is_lat = step < LAT_STEPS
    blk = ATT_BLOCK
    kvw = ATT_HKV * ATT_HD
    tmid = tm_ref[...]
    tabs = [tp_ref[...], tmid[:blk], tmid[blk:], tn_ref[...]]
    masks = [_window_mask(2 * step + sub, is_lat) for sub in range(2)]
    units = []
    for kvh in range(ATT_HKV):
        kc = slice(kvh * ATT_HD, (kvh + 1) * ATT_HD)
        vc = slice(kvw + kvh * ATT_HD, kvw + (kvh + 1) * ATT_HD)
        heads = [kvh * ATT_GRP + g for g in range(ATT_GRP)]
        k_src = [kvp_ref[:, kc], kvm_ref[:blk, kc], kvm_ref[blk:, kc], kvn_ref[:, kc]]
        k_rot = [_rope(k.astype(F32), t).astype(BF16) for k, t in zip(k_src, tabs)]
        v_src = [kvp_ref[:, vc], kvm_ref[:blk, vc], kvm_ref[blk:, vc], kvn_ref[:, vc]]
        for sub in range(2):
            rows = slice(sub * blk, (sub + 1) * blk)
            qs = jnp.concatenate([_rope(q_ref[rows, h * ATT_HD:(h + 1) * ATT_HD].astype(F32), tabs[1 + sub])
                                  for h in heads], axis=0).astype(BF16)
            k_all = jnp.concatenate([kvc_ref[:, kc]] + k_rot[sub:sub + 3], axis=0)
            v_all = jnp.concatenate([kvc_ref[:, vc]] + v_src[sub:sub + 3], axis=0)
            units.append((heads, sub, rows, qs, k_all, v_all))
    scores = [_nt(qs, k_all) for _, _, _, qs, k_all, _ in units]
    probs = []
    for (heads, sub, _, _, _, _), s in zip(units, scores):
        ps, ls = [], []
        for g, h in enumerate(heads):
            s_ctx = s[g * blk:(g + 1) * blk, :CTX_LEN]
            s_loc = jnp.where(masks[sub], s[g * blk:(g + 1) * blk, CTX_LEN:], -jnp.inf)
            sk = sink_ref[h] * LOG2E
            m = jnp.maximum(jnp.maximum(jnp.max(s_ctx, axis=-1, keepdims=True),
                                        jnp.max(s_loc, axis=-1, keepdims=True)), sk)
            p = jnp.concatenate([jnp.exp2(s_ctx - m), jnp.exp2(s_loc - m)], axis=1)
            ls.append(jnp.sum(p, axis=-1, keepdims=True) + jnp.exp2(sk - m))
            ps.append(p.astype(BF16))
        probs.append((jnp.concatenate(ps, axis=0), ls))
    outs = [jnp.dot(p, v_all, preferred_element_type=F32) for (p, _), (_, _, _, _, _, v_all) in zip(probs, units)]
    for (heads, _, rows, _, _, _), (_, ls), out in zip(units, probs, outs):
        for g, h in enumerate(heads):
            hs = slice(h * ATT_HD, (h + 1) * ATT_HD)
            gate = _silu(g_ref[rows, hs].astype(F32))
            o_ref[rows, hs] = (out[g * blk:(g + 1) * blk, :] * (1.0 / ls[g]) * gate).astype(o_ref.dtype)


def _attention(p16, sink, rope_tab):
    blk = ATT_BLOCK
    qw = ATT_HQ * ATT_HD
    kvw = 2 * ATT_HKV * ATT_HD
    q_blk = _P16_OFF["wa_q"] // qw
    g_blk = _P16_OFF["wa_g"] // qw
    kv_blk = _P16_OFF["wa_k"] // kvw
    lat_q = R_LAT // Q_STEP

    def q_rows(b, s):
        return jnp.where(s < LAT_STEPS, LAT_STEPS * b + s, lat_q + CTX_STEPS * b + s - LAT_STEPS)

    def prev_blk(s):
        return jnp.clip(2 * s - 1, 0, LAT_BLKS - 1)

    def next_blk(s):
        return jnp.clip(2 * s + 2, 0, LAT_BLKS - 1)

    ins = [
        pl.BlockSpec(memory_space=pltpu.SMEM),
        pl.BlockSpec((Q_STEP, qw), lambda b, s: (q_rows(b, s), q_blk)),
        pl.BlockSpec((Q_STEP, qw), lambda b, s: (q_rows(b, s), g_blk)),
        pl.BlockSpec((CTX_LEN, kvw), lambda b, s: (R_LAT // CTX_LEN + b, kv_blk)),
        pl.BlockSpec((Q_STEP, kvw), lambda b, s: (q_rows(b, s), kv_blk)),
        pl.BlockSpec((blk, kvw), lambda b, s: (LAT_BLKS * b + prev_blk(s), kv_blk)),
        pl.BlockSpec((blk, kvw), lambda b, s: (LAT_BLKS * b + next_blk(s), kv_blk)),
        pl.BlockSpec((Q_STEP, 2 * ATT_HD), lambda b, s: (s, 0)),
        pl.BlockSpec((blk, 2 * ATT_HD), lambda b, s: (prev_blk(s), 0)),
        pl.BlockSpec((blk, 2 * ATT_HD), lambda b, s: (next_blk(s), 0)),
    ]
    return pl.pallas_call(
        _attn_kernel,
        grid=(BATCH, LAT_STEPS + CTX_STEPS),
        in_specs=ins,
        out_specs=pl.BlockSpec((Q_STEP, qw), lambda b, s: (q_rows(b, s), 0)),
        out_shape=jax.ShapeDtypeStruct((ROWS, qw), BF16),
        compiler_params=pltpu.CompilerParams(vmem_limit_bytes=VMEM_LIMIT),
        name="window_attn",
    )(sink, p16, p16, p16, p16, p16, p16, rope_tab, rope_tab, rope_tab)


def _head_norm(o, gain, heads, width):
    ys = []
    for h in range(heads):
        oh = o[:, h * width:(h + 1) * width]
        ys.append(oh * lax.rsqrt(jnp.mean(oh * oh, axis=-1, keepdims=True) + EPS) * gain)
    return jnp.concatenate(ys, axis=1)


def _merge_kernel(final, ogf_ref, ogb_ref, gg_ref, ohf_ref, ohb_ref, hg_ref, ya_ref, mg_ref, x_ref,
                  mod_ref, gn_gla_ref, gn_hg_ref, wbr_ref, wout_ref, *rest):
    for r in range(x_ref.shape[0] // MERGE_ROWS):
        rows = slice(r * MERGE_ROWS, (r + 1) * MERGE_ROWS)
        o_gla = ogf_ref[rows, :].astype(F32) + ogb_ref[rows, :].astype(F32)
        o_hg = ohf_ref[rows, :].astype(F32) + ohb_ref[rows, :].astype(F32)
        y_gla = _head_norm(o_gla, gn_gla_ref[...], GLA_HEADS, GLA_HV) * _silu(gg_ref[rows, :].astype(F32))
        y_hg = _head_norm(o_hg, gn_hg_ref[...], HGRN_HEADS, HGRN_HV) * _silu(hg_ref[rows, :].astype(F32))
        ys = (y_gla.astype(BF16), y_hg.astype(BF16), ya_ref[rows, :])
        merged = None
        for n in range(N_BRANCH):
            proj = jnp.dot(ys[n], wbr_ref[n], preferred_element_type=F32)
            term = jax.nn.sigmoid(mg_ref[rows, n * D_MODEL:(n + 1) * D_MODEL].astype(F32)) * proj
            merged = term if merged is None else merged + term
        upd = jnp.dot(merged.astype(BF16), wout_ref[...], preferred_element_type=F32)
        x_new = x_ref[rows, :] + mod_ref[0][:, 2 * D_MODEL:] * upd
        if final:
            fg_ref, out_ref = rest
            out_ref[rows, :] = (x_new * lax.rsqrt(jnp.mean(x_new * x_new, axis=-1, keepdims=True) + EPS)
                                * fg_ref[...])
        else:
            ng_ref, nmod_ref, xo_ref, h_ref = rest
            xo_ref[rows, :] = x_new
            h_ref[rows, :] = _norm_mod(x_new, ng_ref[...], nmod_ref[0])


def _merge(final, context, og, oh, y_att, p16, x2d, mod_l, gn_gla, gn_hg, wbr, wout, tail):
    tm = CTX_TILE if context else LAT_TILE
    rows = x2d.shape[0]
    blk0 = R_LAT // tm if context else 0
    mod_row = _ctx_mod_row if context else _lat_mod_row

    def shared(width, col=0):
        return pl.BlockSpec((tm, width), lambda t: (blk0 + t, col))

    def whole(shape):
        return pl.BlockSpec(shape, lambda t: (0,) * len(shape), pipeline_mode=pl.Buffered(1))

    ins = [
        shared(1024), shared(1024), shared(1024, _P16_OFF["ga_g"] // 1024),
        shared(1024), shared(1024), shared(1024, _P16_OFF["hg_g"] // 1024),
        shared(1024), shared(3072, _P16_OFF["mg"] // 3072),
        pl.BlockSpec((tm, D_MODEL), lambda t: (t, 0)),
        pl.BlockSpec((1, 1, 3 * D_MODEL), lambda t: (mod_row(t), 0, 0)),
        whole((1, GLA_HV)), whole((1, HGRN_HV)),
        whole((N_BRANCH, D_MODEL, D_MODEL)), whole((D_MODEL, D_MODEL)),
        whole((1, D_MODEL)),
    ]
    args = [og[0], og[1], p16, oh[0], oh[1], p16, y_att, p16, x2d, mod_l,
            gn_gla.reshape(1, GLA_HV), gn_hg.reshape(1, HGRN_HV), wbr, wout, tail[0].reshape(1, D_MODEL)]
    row_spec = pl.BlockSpec((tm, D_MODEL), lambda t: (t, 0))
    if final:
        out_specs = row_spec
        out_shape = jax.ShapeDtypeStruct((rows, D_MODEL), F32)
    else:
        ins.append(pl.BlockSpec((1, 1, 3 * D_MODEL), lambda t: (mod_row(t), 0, 0)))
        args.append(tail[1])
        out_specs = [row_spec, row_spec]
        out_shape = [jax.ShapeDtypeStruct((rows, D_MODEL), F32), jax.ShapeDtypeStruct((rows, D_MODEL), BF16)]
    return pl.pallas_call(
        functools.partial(_merge_kernel, final),
        grid=(rows // tm,),
        in_specs=ins,
        out_specs=out_specs,
        out_shape=out_shape,
        compiler_params=pltpu.CompilerParams(vmem_limit_bytes=VMEM_LIMIT),
        name="merge_final" if final else ("merge_ctx" if context else "merge"),
    )(*args)


def _rope_table():
    r = ATT_HD // 4
    inv = ROPE_BASE ** (-jnp.arange(r, dtype=F32) / r)
    t = jnp.arange(SEQ)
    ang_row = (t // GRID_W).astype(F32)[:, None] * inv
    ang_col = (t % GRID_W).astype(F32)[:, None] * inv
    cos = jnp.concatenate([jnp.cos(ang_row)] * 2 + [jnp.cos(ang_col)] * 2, axis=1)
    sin = jnp.concatenate([-jnp.sin(ang_row), jnp.sin(ang_row), -jnp.sin(ang_col), jnp.sin(ang_col)], axis=1)
    tab = jnp.concatenate([cos, sin], axis=1)
    ident = jnp.concatenate([jnp.ones((CTX_LEN, ATT_HD), F32), jnp.zeros((CTX_LEN, ATT_HD), F32)], axis=1)
    return jnp.concatenate([tab, ident], axis=0)


def _cols(w, name):
    o, s = _IN_OFF[name]
    return w[:, o:o + s] * _Q_SCALE.get(name, 1.0)


def kernel(x, c, ctx, c_ctx, norm_g, w_ada, b_ada, w_in, gla_w_a2, gla_b_a2, gla_norm_g, hgrn_lb_logits,
           hgrn_norm_g, attn_sink, w_branch, w_out, final_g):
    cc = jnp.zeros((16, D_MODEL), F32).at[:BATCH].set(c).at[BATCH].set(c_ctx)
    mod = _modulation(cc, w_ada, b_ada).reshape(DEPTH, 16, 1, 3 * D_MODEL)
    rope_tab = _rope_table()
    lb_cum = jnp.cumsum(jax.nn.softmax(hgrn_lb_logits.astype(F32), axis=0), axis=0)
    lower_bounds = lb_cum - lb_cum[0]

    xl = x.reshape(R_LAT, D_MODEL)
    xc = ctx.reshape(R_CTX, D_MODEL)
    hl = _norm_modulate(xl, norm_g[0], mod[0], LAT_TILE, _lat_mod_row)
    hc = _norm_modulate(xc, norm_g[0], mod[0], CTX_TILE, _ctx_mod_row)
    out = None
    for l in range(DEPTH):
        final = l == DEPTH - 1
        w = w_in[l]
        w16 = jnp.concatenate([_cols(w, n) for n in _P16_ORDER], axis=1).astype(BF16)
        w_f = _cols(w, "hg_f").astype(BF16)
        w_lr = jnp.concatenate([_cols(w, "ga_lr"), jnp.zeros((D_MODEL, LR_PAD - 2 * GLA_RANK), F32)],
                               axis=1).astype(BF16)
        wz = jnp.zeros((LR_PAD, 2 * GLA_HEADS * GLA_HK), F32)
        wz = wz.at[:GLA_RANK, :GLA_HEADS * GLA_HK].set(gla_w_a2[l, 0])
        wz = wz.at[GLA_RANK:2 * GLA_RANK, GLA_HEADS * GLA_HK:].set(gla_w_a2[l, 1]).astype(BF16)
        bz = gla_b_a2[l].reshape(1, 2 * GLA_HEADS * GLA_HK)
        lb = lower_bounds[l]
        lb_logs = jnp.stack([jnp.log(lb), jnp.log1p(-lb)], axis=0).reshape(2, 2 * HGRN_F)

        p16 = _in_proj(hl, hc, w16, BF16, 512)
        p_lr = _in_proj(hl, hc, w_lr, F32, LR_PAD)
        log_f = _forget_proj(hl, hc, w_f, lb_logs, l == 0)
        og = _gla(p16, p_lr, wz, bz)
        oh = _hgrn(p16, log_f)
        y_att = _attention(p16, attn_sink[l], rope_tab)
        common = (og, oh, y_att, p16)
        params = (mod[l], gla_norm_g[l], hgrn_norm_g[l], w_branch[l].astype(BF16), w_out[l].astype(BF16))
        if final:
            out = _merge(True, False, *common, xl, *params, (final_g,))
        else:
            tail = (norm_g[l + 1], mod[l + 1])
            xl, hl = _merge(False, False, *common, xl, *params, tail)
            xc, hc = _merge(False, True, *common, xc, *params, tail)
    return out.reshape(BATCH, SEQ, D_MODEL)
```

```python
import functools

import numpy as np
import jax
import jax.numpy as jnp
from jax import lax
from jax.experimental import pallas as pl
from jax.experimental.pallas import tpu as pltpu

F32 = jnp.float32
BF16 = jnp.bfloat16

D_MODEL = 1024
BATCH = 8
SEQ = 4096
DEPTH = 2
CTX_LEN = 256
GRID_W = 64
N_BRANCH = 3
GLA_HEADS = 4
GLA_HV = 256
GLA_HK = 128
GLA_RANK = 16
GLA_TAU = 16.0
HGRN_HEADS = 8
HGRN_HV = 128
HGRN_HK = 128
HGRN_F = HGRN_HEADS * HGRN_HK
ATT_HD = 128
ATT_HQ = 8
ATT_HKV = 2
ATT_GRP = ATT_HQ // ATT_HKV
WINDOW = 128
ATT_BLOCK = 128
ROPE_BASE = 10000.0
EPS = 1e-6

R_LAT = BATCH * SEQ
R_CTX = BATCH * CTX_LEN
ROWS = R_LAT + R_CTX

_IN_SIZES = (512, 512, 1024, 1024, 32, 1024, 2048, 1024, 1024, 1024, 256, 256, 1024, 3072)
_IN_NAMES = ("ga_q", "ga_k", "ga_v", "ga_g", "ga_lr", "hg_q", "hg_f", "hg_i", "hg_g",
             "wa_q", "wa_k", "wa_v", "wa_g", "mg")
_IN_OFF = {}
_o = 0
for _n, _s in zip(_IN_NAMES, _IN_SIZES):
    _IN_OFF[_n] = (_o, _s)
    _o += _s

_P16_ORDER = ("mg", "ga_v", "ga_g", "hg_q", "hg_i", "hg_g", "wa_q", "wa_g", "ga_q", "ga_k", "wa_k", "wa_v")
_P16_OFF = {}
_o = 0
for _n in _P16_ORDER:
    _P16_OFF[_n] = _o
    _o += _IN_OFF[_n][1]
P16_COLS = _o
LR_PAD = 256
LOG2E = 1.4426950408889634
_Q_SCALE = {"ga_q": GLA_HK ** -0.5, "wa_q": ATT_HD ** -0.5 * LOG2E}

VMEM_LIMIT = 56 * 1024 * 1024


def _nt(a, b):
    return lax.dot_general(a, b, (((1,), (1,)), ((), ())), preferred_element_type=F32)


def _tn(a, b):
    return lax.dot_general(a, b, (((0,), (0,)), ((), ())), preferred_element_type=F32)


def _log_sigmoid(z):
    return jnp.minimum(z, 0.0) - jnp.log(1.0 + jnp.exp(-jnp.abs(z)))


def _silu(x):
    return x * jax.nn.sigmoid(x)


def _norm_mod(x, g, m):
    y = x * lax.rsqrt(jnp.mean(x * x, axis=-1, keepdims=True) + EPS) * g
    return (y * (1.0 + m[:, D_MODEL:2 * D_MODEL]) + m[:, :D_MODEL]).astype(BF16)


def _mod_kernel(c_ref, w_ref, b_ref, o_ref):
    a = _silu(c_ref[...])
    o_ref[0] = jnp.dot(a.astype(BF16), w_ref[0].astype(BF16), preferred_element_type=F32) + b_ref[0]


def _modulation(cc, w_ada, b_ada):
    tn = 512
    return pl.pallas_call(
        _mod_kernel,
        grid=(DEPTH, 3 * D_MODEL // tn),
        in_specs=[
            pl.BlockSpec((16, D_MODEL), lambda l, j: (0, 0)),
            pl.BlockSpec((1, D_MODEL, tn), lambda l, j: (l, 0, j)),
            pl.BlockSpec((1, 1, tn), lambda l, j: (l, 0, j)),
        ],
        out_specs=pl.BlockSpec((1, 16, tn), lambda l, j: (l, 0, j)),
        out_shape=jax.ShapeDtypeStruct((DEPTH, 16, 3 * D_MODEL), F32),
        name="modulation",
    )(cc, w_ada, b_ada.reshape(DEPTH, 1, 3 * D_MODEL))


LAT_TILE = 512
CTX_TILE = 256
MERGE_ROWS = 256


def _lat_mod_row(t):
    return t // (SEQ // LAT_TILE)


def _ctx_mod_row(t):
    return BATCH


def _norm_kernel(x_ref, g_ref, mod_ref, h_ref):
    h_ref[...] = _norm_mod(x_ref[...], g_ref[...], mod_ref[0])


def _norm_modulate(x2d, norm_g, mod_l, tm, mod_row):
    rows = x2d.shape[0]
    return pl.pallas_call(
        _norm_kernel,
        grid=(rows // tm,),
        in_specs=[
            pl.BlockSpec((tm, D_MODEL), lambda t: (t, 0)),
            pl.BlockSpec((1, D_MODEL), lambda t: (0, 0)),
            pl.BlockSpec((1, 1, 3 * D_MODEL), lambda t: (mod_row(t), 0, 0)),
        ],
        out_specs=pl.BlockSpec((tm, D_MODEL), lambda t: (t, 0)),
        out_shape=jax.ShapeDtypeStruct((rows, D_MODEL), BF16),
        name="norm_modulate",
    )(x2d, norm_g.reshape(1, D_MODEL), mod_l)


PROJ_TILE = 2048
N_LAT_PROJ = R_LAT // PROJ_TILE
N_CTX_PROJ = R_CTX // PROJ_TILE


def _mm_kernel(hl_ref, hc_ref, w_ref, o_ref):
    i = pl.program_id(0)

    @pl.when(i < N_LAT_PROJ)
    def _():
        o_ref[...] = jnp.dot(hl_ref[...], w_ref[...], preferred_element_type=F32).astype(o_ref.dtype)

    @pl.when(i >= N_LAT_PROJ)
    def _():
        o_ref[...] = jnp.dot(hc_ref[...], w_ref[...], preferred_element_type=F32).astype(o_ref.dtype)


def _proj_specs(k, tn):
    tm = PROJ_TILE
    return [
        pl.BlockSpec((tm, k), lambda i, j: (jnp.minimum(i, N_LAT_PROJ - 1), 0)),
        pl.BlockSpec((tm, k), lambda i, j: (jnp.maximum(i - N_LAT_PROJ, 0), 0)),
        pl.BlockSpec((k, tn), lambda i, j: (0, j)),
    ]


def _in_proj(hl, hc, w, out_dtype, tn):
    k, n = w.shape
    return pl.pallas_call(
        _mm_kernel,
        grid=(N_LAT_PROJ + N_CTX_PROJ, n // tn),
        in_specs=_proj_specs(k, tn),
        out_specs=pl.BlockSpec((PROJ_TILE, tn), lambda i, j: (i, j)),
        out_shape=jax.ShapeDtypeStruct((ROWS, n), out_dtype),
        compiler_params=pltpu.CompilerParams(vmem_limit_bytes=VMEM_LIMIT),
        name="in_proj",
    )(hl, hc, w)


GATE_ROWS = 256


def _forget_proj_kernel(zero_bound, hl_ref, hc_ref, w_ref, lb_ref, o_ref):
    i = pl.program_id(0)
    lb = lb_ref[...]

    def run(h_ref):
        for r in range(PROJ_TILE // GATE_ROWS):
            rows = slice(r * GATE_ROWS, (r + 1) * GATE_ROWS)
            z = jnp.dot(h_ref[rows, :], w_ref[...], preferred_element_type=F32)
            t = jnp.exp(-jnp.abs(z))
            log_num = jnp.minimum(z, 0.0)
            if not zero_bound:
                log_num = jnp.maximum(jnp.log(jnp.where(z >= 0.0, 1.0 + lb * t, lb + t)), log_num)
            o_ref[rows, :] = log_num - jnp.log(1.0 + t)

    @pl.when(i < N_LAT_PROJ)
    def _():
        run(hl_ref)

    @pl.when(i >= N_LAT_PROJ)
    def _():
        run(hc_ref)


def _forget_proj(hl, hc, w, lb_row, zero_bound):
    k, n = w.shape
    tn = 256
    return pl.pallas_call(
        functools.partial(_forget_proj_kernel, zero_bound),
        grid=(N_LAT_PROJ + N_CTX_PROJ, n // tn),
        in_specs=_proj_specs(k, tn) + [pl.BlockSpec((1, tn), lambda i, j: (0, j))],
        out_specs=pl.BlockSpec((PROJ_TILE, tn), lambda i, j: (i, j)),
        out_shape=jax.ShapeDtypeStruct((ROWS, n), F32),
        compiler_params=pltpu.CompilerParams(vmem_limit_bytes=VMEM_LIMIT),
        name="forget_proj",
    )(hl, hc, w, lb_row)


CHUNK = 128
SUB = 16
N_SUB = CHUNK // SUB
STEP_CHUNKS = 2
STEP_ROWS = STEP_CHUNKS * CHUNK
LAT_CHUNKS = SEQ // STEP_ROWS
CTX_CHUNKS = CTX_LEN // STEP_ROWS
N_CHUNK = LAT_CHUNKS + CTX_CHUNKS
CTX_CHUNK0 = R_LAT // STEP_ROWS

_PAIR_LEVELS = tuple(SUB * 2 ** i for i in range(N_SUB.bit_length() - 1))
_GROUPS = tuple(2 ** (i + 1) for i in range(N_SUB.bit_length() - 1))
_FAC_ROWS = -(-8 * (2 * len(_GROUPS) + 1) // 16) * 16


def _fwd_rows(b, n):
    return jnp.where(n < CTX_CHUNKS, CTX_CHUNK0 + CTX_CHUNKS * b + n, LAT_CHUNKS * b + n - CTX_CHUNKS)


def _bwd_rows(b, n):
    return jnp.where(n < CTX_CHUNKS, CTX_CHUNK0 + CTX_CHUNKS * b + CTX_CHUNKS - 1 - n,
                     LAT_CHUNKS * b + N_CHUNK - 1 - n)


def _level_matrix(reverse):
    nb = N_SUB
    pos = [nb - 1 - m for m in range(nb)] if reverse else list(range(nb))
    mat = np.zeros((_FAC_ROWS, 16), np.float32)
    for gi, r in enumerate(_GROUPS):
        for m in range(nb):
            for m2 in range(nb):
                if m2 // r == m // r:
                    if pos[m2] < pos[m]:
                        mat[8 * gi + m, m2] = 1.0
                    if pos[m2] > pos[m]:
                        mat[8 * (len(_GROUPS) + gi) + m, m2] = 1.0
    mat[8 * 2 * len(_GROUPS), :nb] = 1.0
    return mat


def _cumsum_matrix(reverse):
    i = np.arange(CHUNK)[:, None]
    j = np.arange(CHUNK)[None, :]
    t = ((i // SUB == j // SUB) & ((j >= i) if reverse else (j <= i))).astype(np.float32)
    return np.concatenate([t, t], axis=1)


def _mxu_cumsum(la, cmat):
    hi = la.astype(BF16)
    lo = (la - hi.astype(F32)).astype(BF16)
    return jnp.dot(cmat, jnp.concatenate([hi, lo], axis=0), preferred_element_type=F32)


def _rep_rows(x):
    return jnp.concatenate([jnp.broadcast_to(x[m:m + 1], (SUB, x.shape[1])) for m in range(N_SUB)], axis=0)


def _split3(x):
    hi = x.astype(BF16)
    r1 = x - hi.astype(F32)
    mid = r1.astype(BF16)
    lo = (r1 - mid.astype(F32)).astype(BF16)
    return hi, mid, lo


def _block_factors(bl, lmat, reverse):
    last = [SUB * m + (0 if reverse else SUB - 1) for m in range(N_SUB)]
    tot = jnp.concatenate([bl[i:i + 1] for i in last], axis=0)
    tot16 = jnp.concatenate([tot, jnp.zeros((16 - N_SUB, tot.shape[1]), F32)], axis=0)
    logs = sum(jnp.dot(lmat, t, preferred_element_type=F32) for t in _split3(tot16))
    return tot, jnp.exp(logs)


def _decay_variants(q, k, bl, tot, fac):
    qe = q * jnp.exp(bl)
    ke = k * jnp.exp(_rep_rows(tot) - bl)
    q_lv = {SUB: qe.astype(BF16)}
    k_lv = {SUB: ke.astype(BF16), "diag": (k * jnp.exp(-bl)).astype(BF16)}
    for gi in range(len(_GROUPS)):
        s = 2 * SUB * 2 ** gi
        q_lv[s] = (qe * _rep_rows(fac[8 * gi:8 * gi + N_SUB])).astype(BF16)
        fo = 8 * (len(_GROUPS) + gi)
        k_lv[s] = (ke * _rep_rows(fac[fo:fo + N_SUB])).astype(BF16)
    decay = fac[8 * 2 * len(_GROUPS):8 * 2 * len(_GROUPS) + 1]
    return q_lv, k_lv, decay


def _lead_rows(x, s, reverse):
    first = 0 if reverse else 1
    pieces = [x[(2 * g + first) * s:(2 * g + first + 1) * s] for g in range(CHUNK // (2 * s))]
    return pieces[0] if len(pieces) == 1 else jnp.concatenate(pieces, axis=0)


def _scan_masks(reverse):
    n = CHUNK
    i = lax.broadcasted_iota(jnp.int32, (n, n), 0)
    j = lax.broadcasted_iota(jnp.int32, (n, n), 1)
    sh = SUB.bit_length() - 1
    same = jnp.right_shift(i, sh) == jnp.right_shift(j, sh)
    masks = {"diag": jnp.logical_and(same, j >= i if reverse else j <= i)}
    rc = lax.broadcasted_iota(jnp.int32, (n // 2, n), 0)
    jc = lax.broadcasted_iota(jnp.int32, (n // 2, n), 1)
    for s in _PAIR_LEVELS:
        sh = s.bit_length() - 1
        masks[s] = jnp.right_shift(jc, sh) == 2 * jnp.right_shift(rc, sh) + (1 if reverse else 0)
    return masks


def _intra_scores(q_lv, k_lv, cols, masks, reverse):
    diag = jnp.where(masks["diag"], _nt(q_lv[SUB][:, cols], k_lv["diag"][:, cols]), 0.0)
    lead = {}
    for s in _PAIR_LEVELS:
        lead[s] = jnp.where(masks[s], _nt(_lead_rows(q_lv[s][:, cols], s, reverse), k_lv[s][:, cols]), 0.0)
    blocks = []
    for m in range(N_SUB):
        blk = diag[SUB * m:SUB * (m + 1)]
        for s in _PAIR_LEVELS:
            r = s // SUB
            ms = m // r
            if ms % 2 == (0 if reverse else 1):
                off = (ms // 2) * s + (m % r) * SUB
                blk = blk + lead[s][off:off + SUB]
        blocks.append(blk)
    return jnp.concatenate(blocks, axis=0).astype(BF16)


def _step_chunks(reverse):
    order = range(STEP_CHUNKS - 1, -1, -1) if reverse else range(STEP_CHUNKS)
    return [slice(c * CHUNK, (c + 1) * CHUNK) for c in order]


def _scan_step(directions, heads, hk, hv):
    chunks = [(d, rows) for d in directions for rows in _step_chunks(d[6])]
    bls = [d[0](rows) for d, rows in chunks]
    facs = [_block_factors(bl, d[2], d[6]) for (d, _), bl in zip(chunks, bls)]
    variants = [_decay_variants(*d[1](rows), bl, tot, fac) for (d, rows), bl, (tot, fac) in zip(chunks, bls, facs)]
    masks = {d[6]: _scan_masks(d[6]) for d in directions}
    heads_of = [(slice(h * hk, (h + 1) * hk), slice(h * hv, (h + 1) * hv)) for h in range(heads)]
    atts = [[_intra_scores(q_lv, k_lv, ks, masks[d[6]], d[6]) for ks, _ in heads_of]
            for (d, _), (q_lv, k_lv, _) in zip(chunks, variants)]
    intra = [[jnp.dot(a, d[3][rows, vs], preferred_element_type=F32) for a, (_, vs) in zip(att, heads_of)]
             for (d, rows), att in zip(chunks, atts)]
    for (d, rows), (q_lv, k_lv, decay), o_in in zip(chunks, variants, intra):
        _, _, _, v_ref, o_ref, st_ref, _ = d
        for h, (ks, vs) in enumerate(heads_of):
            st = st_ref[h]
            o_ref[rows, vs] = (o_in[h] + _nt(q_lv[CHUNK][:, ks], st.astype(BF16))).astype(o_ref.dtype)
            st_ref[h] = decay[:, ks] * st + _tn(v_ref[rows, vs], k_lv[CHUNK][:, ks])


def _reset_state(stf_ref, stb_ref):
    @pl.when(pl.program_id(1) == 0)
    def _():
        stf_ref[...] = jnp.zeros_like(stf_ref)
        stb_ref[...] = jnp.zeros_like(stb_ref)


def _gla_kernel(qf_ref, kf_ref, vf_ref, lrf_ref, qb_ref, kb_ref, vb_ref, lrb_ref, wz_ref, bz_ref,
                cmat_ref, lmat_ref, of_ref, ob_ref, stf_ref, stb_ref):
    _reset_state(stf_ref, stb_ref)
    width = GLA_HEADS * GLA_HK
    gates = [jnp.dot(lr_ref[...].astype(BF16), wz_ref[:, d * width:(d + 1) * width], preferred_element_type=F32)
             + bz_ref[:, d * width:(d + 1) * width] for d, lr_ref in enumerate((lrf_ref, lrb_ref))]
    directions = []
    for d, (q_ref, k_ref, v_ref, o_ref, st_ref) in enumerate((
            (qf_ref, kf_ref, vf_ref, of_ref, stf_ref),
            (qb_ref, kb_ref, vb_ref, ob_ref, stb_ref))):
        reverse = d == 1

        def bl_of(rows, d=d):
            return _mxu_cumsum(_log_sigmoid(gates[d][rows]) / GLA_TAU, cmat_ref[d])

        def qk_of(rows, q_ref=q_ref, k_ref=k_ref):
            return q_ref[rows, :].astype(F32), k_ref[rows, :].astype(F32)

        directions.append((bl_of, qk_of, lmat_ref[d], v_ref, o_ref, st_ref, reverse))
    _scan_step(directions, GLA_HEADS, GLA_HK, GLA_HV)


def _scan_call(kernel_fn, ins, args, name, heads, hk, hv):
    out_sds = jax.ShapeDtypeStruct((ROWS, heads * hv), BF16)
    consts = [np.stack([_cumsum_matrix(False), _cumsum_matrix(True)]),
              np.stack([_level_matrix(False), _level_matrix(True)])]
    return pl.pallas_call(
        kernel_fn,
        grid=(BATCH, N_CHUNK),
        in_specs=ins + [pl.BlockSpec(c.shape, lambda b, n: (0, 0, 0)) for c in consts],
        out_specs=[pl.BlockSpec((STEP_ROWS, heads * hv), lambda b, n: (_fwd_rows(b, n), 0)),
                   pl.BlockSpec((STEP_ROWS, heads * hv), lambda b, n: (_bwd_rows(b, n), 0))],
        out_shape=[out_sds, out_sds],
        scratch_shapes=[pltpu.VMEM((heads, hv, hk), F32), pltpu.VMEM((heads, hv, hk), F32)],
        compiler_params=pltpu.CompilerParams(dimension_semantics=("arbitrary", "arbitrary"),
                                             vmem_limit_bytes=VMEM_LIMIT),
        name=name,
    )(*args, *[jnp.asarray(c, dtype=BF16) for c in consts])


def _chunk_spec(width, blk, rows_of):
    return pl.BlockSpec((STEP_ROWS, width), lambda b, n: (rows_of(b, n), blk))


def _gla(p16, p_lr, wz, bz):
    q_blk = _P16_OFF["ga_q"] // 512
    k_blk = _P16_OFF["ga_k"] // 512
    v_blk = _P16_OFF["ga_v"] // 1024
    ins, args = [], []
    for rows_of in (_fwd_rows, _bwd_rows):
        ins += [_chunk_spec(512, q_blk, rows_of), _chunk_spec(512, k_blk, rows_of),
                _chunk_spec(1024, v_blk, rows_of), _chunk_spec(LR_PAD, 0, rows_of)]
        args += [p16, p16, p16, p_lr]
    ins += [pl.BlockSpec((LR_PAD, 1024), lambda b, n: (0, 0)), pl.BlockSpec((1, 1024), lambda b, n: (0, 0))]
    args += [wz, bz]
    return _scan_call(_gla_kernel, ins, args, "gla_scan", GLA_HEADS, GLA_HK, GLA_HV)


def _hgrn_kernel(qf_ref, if_ref, ff_ref, qb_ref, ib_ref, fb_ref, cmat_ref, lmat_ref, of_ref, ob_ref,
                 stf_ref, stb_ref):
    _reset_state(stf_ref, stb_ref)
    directions = []
    for d, (q_ref, i_ref, f_ref, o_ref, st_ref) in enumerate((
            (qf_ref, if_ref, ff_ref, of_ref, stf_ref),
            (qb_ref, ib_ref, fb_ref, ob_ref, stb_ref))):

        def bl_of(rows, d=d, f_ref=f_ref):
            return _mxu_cumsum(f_ref[rows, :], cmat_ref[d])

        def qk_of(rows, q_ref=q_ref, f_ref=f_ref):
            return q_ref[rows, :].astype(F32), 1.0 - jnp.exp(f_ref[rows, :])

        directions.append((bl_of, qk_of, lmat_ref[d], i_ref, o_ref, st_ref, d == 1))
    _scan_step(directions, HGRN_HEADS, HGRN_HK, HGRN_HV)


def _hgrn(p16, log_f):
    q_blk = _P16_OFF["hg_q"] // 1024
    i_blk = _P16_OFF["hg_i"] // 1024
    ins, args = [], []
    for d, rows_of in enumerate((_fwd_rows, _bwd_rows)):
        ins += [_chunk_spec(1024, blk, rows_of) for blk in (q_blk, i_blk, d)]
        args += [p16, p16, log_f]
    return _scan_call(_hgrn_kernel, ins, args, "hgrn_scan", HGRN_HEADS, HGRN_HK, HGRN_HV)


Q_STEP = 2 * ATT_BLOCK
LAT_BLKS = SEQ // ATT_BLOCK
LAT_STEPS = SEQ // Q_STEP
CTX_STEPS = CTX_LEN // Q_STEP
_FAR = 4 * ATT_BLOCK


def _rope(x, tab):
    lane = lax.broadcasted_iota(jnp.int32, x.shape, 1)
    partner = jnp.where(jnp.bitwise_and(lane, 32) == 0, pltpu.roll(x, 96, 1), pltpu.roll(x, 32, 1))
    return x * tab[:, :ATT_HD] + partner * tab[:, ATT_HD:]


def _window_mask(jblk, is_lat):
    blk = ATT_BLOCK
    off_prev = jnp.where(jnp.logical_and(is_lat, jblk >= 1), 0, _FAR)
    off_cur = jnp.where(is_lat, 0, _FAR)
    off_next = jnp.where(jnp.logical_and(is_lat, jblk + 1 < LAT_BLKS), 0, _FAR)
    qi = lax.broadcasted_iota(jnp.int32, (blk, 3 * blk), 0)
    col = lax.broadcasted_iota(jnp.int32, (blk, 3 * blk), 1)
    land, lor = jnp.logical_and, jnp.logical_or
    in_prev = land(col < blk, col >= qi + off_prev)
    in_cur = land(col >= blk + off_cur, col < 2 * blk)
    in_next = land(col >= 2 * blk, col - 2 * blk + off_next <= qi)
    return lor(in_prev, lor(in_cur, in_next))


def _attn_kernel(sink_ref, q_ref, g_ref, kvc_ref, kvm_ref, kvp_ref, kvn_ref, tm_ref, tp_ref, tn_ref, o_ref):
    step = pl.program_id(1)
    is_lat = step < LAT_STEPS
    blk = ATT_BLOCK
    kvw = ATT_HKV * ATT_HD
    tmid = tm_ref[...]
    tabs = [tp_ref[...], tmid[:blk], tmid[blk:], tn_ref[...]]
    masks = [_window_mask(2 * step + sub, is_lat) for sub in range(2)]
    units = []
    for kvh in range(ATT_HKV):
        kc = slice(kvh * ATT_HD, (kvh + 1) * ATT_HD)
        vc = slice(kvw + kvh * ATT_HD, kvw + (kvh + 1) * ATT_HD)
        heads = [kvh * ATT_GRP + g for g in range(ATT_GRP)]
        k_src = [kvp_ref[:, kc], kvm_ref[:blk, kc], kvm_ref[blk:, kc], kvn_ref[:, kc]]
        k_rot = [_rope(k.astype(F32), t).astype(BF16) for k, t in zip(k_src, tabs)]
        v_src = [kvp_ref[:, vc], kvm_ref[:blk, vc], kvm_ref[blk:, vc], kvn_ref[:, vc]]
        for sub in range(2):
            rows = slice(sub * blk, (sub + 1) * blk)
            qs = jnp.concatenate([_rope(q_ref[rows, h * ATT_HD:(h + 1) * ATT_HD].astype(F32), tabs[1 + sub])
                                  for h in heads], axis=0).astype(BF16)
            k_all = jnp.concatenate([kvc_ref[:, kc]] + k_rot[sub:sub + 3], axis=0)
            v_all = jnp.concatenate([kvc_ref[:, vc]] + v_src[sub:sub + 3], axis=0)
            units.append((heads, sub, rows, qs, k_all, v_all))
    scores = [_nt(qs, k_all) for _, _, _, qs, k_all, _ in units]
    probs = []
    for (heads, sub, _, _, _, _), s in zip(units, scores):
        ps, ls = [], []
        for g, h in enumerate(heads):
            s_ctx = s[g * blk:(g + 1) * blk, :CTX_LEN]
            s_loc = jnp.where(masks[sub], s[g * blk:(g + 1) * blk, CTX_LEN:], -jnp.inf)
            sk = sink_ref[h] * LOG2E
            m = jnp.maximum(jnp.maximum(jnp.max(s_ctx, axis=-1, keepdims=True),
                                        jnp.max(s_loc, axis=-1, keepdims=True)), sk)
            p = jnp.concatenate([jnp.exp2(s_ctx - m), jnp.exp2(s_loc - m)], axis=1)
            ls.append(jnp.sum(p, axis=-1, keepdims=True) + jnp.exp2(sk - m))
            ps.append(p.astype(BF16))
        probs.append((jnp.concatenate(ps, axis=0), ls))
    outs = [jnp.dot(p, v_all, preferred_element_type=F32) for (p, _), (_, _, _, _, _, v_all) in zip(probs, units)]
    for (heads, _, rows, _, _, _), (_, ls), out in zip(units, probs, outs):
        for g, h in enumerate(heads):
            hs = slice(h * ATT_HD, (h + 1) * ATT_HD)
            gate = _silu(g_ref[rows, hs].astype(F32))
            o_ref[rows, hs] = (out[g * blk:(g + 1) * blk, :] * (1.0 / ls[g]) * gate).astype(o_ref.dtype)


def _attention(p16, sink, rope_tab):
    blk = ATT_BLOCK
    qw = ATT_HQ * ATT_HD
    kvw = 2 * ATT_HKV * ATT_HD
    q_blk = _P16_OFF["wa_q"] // qw
    g_blk = _P16_OFF["wa_g"] // qw
    kv_blk = _P16_OFF["wa_k"] // kvw
    lat_q = R_LAT // Q_STEP

    def q_rows(b, s):
        return jnp.where(s < LAT_STEPS, LAT_STEPS * b + s, lat_q + CTX_STEPS * b + s - LAT_STEPS)

    def prev_blk(s):
        return jnp.clip(2 * s - 1, 0, LAT_BLKS - 1)

    def next_blk(s):
        return jnp.clip(2 * s + 2, 0, LAT_BLKS - 1)

    ins = [
        pl.BlockSpec(memory_space=pltpu.SMEM),
        pl.BlockSpec((Q_STEP, qw), lambda b, s: (q_rows(b, s), q_blk)),
        pl.BlockSpec((Q_STEP, qw), lambda b, s: (q_rows(b, s), g_blk)),
        pl.BlockSpec((CTX_LEN, kvw), lambda b, s: (R_LAT // CTX_LEN + b, kv_blk)),
        pl.BlockSpec((Q_STEP, kvw), lambda b, s: (q_rows(b, s), kv_blk)),
        pl.BlockSpec((blk, kvw), lambda b, s: (LAT_BLKS * b + prev_blk(s), kv_blk)),
        pl.BlockSpec((blk, kvw), lambda b, s: (LAT_BLKS * b + next_blk(s), kv_blk)),
        pl.BlockSpec((Q_STEP, 2 * ATT_HD), lambda b, s: (s, 0)),
        pl.BlockSpec((blk, 2 * ATT_HD), lambda b, s: (prev_blk(s), 0)),
        pl.BlockSpec((blk, 2 * ATT_HD), lambda b, s: (next_blk(s), 0)),
    ]
    return pl.pallas_call(
        _attn_kernel,
        grid=(BATCH, LAT_STEPS + CTX_STEPS),
        in_specs=ins,
        out_specs=pl.BlockSpec((Q_STEP, qw), lambda b, s: (q_rows(b, s), 0)),
        out_shape=jax.ShapeDtypeStruct((ROWS, qw), BF16),
        compiler_params=pltpu.CompilerParams(vmem_limit_bytes=VMEM_LIMIT),
        name="window_attn",
    )(sink, p16, p16, p16, p16, p16, p16, rope_tab, rope_tab, rope_tab)


def _head_norm(o, gain, heads, width):
    ys = []
    for h in range(heads):
        oh = o[:, h * width:(h + 1) * width]
        ys.append(oh * lax.rsqrt(jnp.mean(oh * oh, axis=-1, keepdims=True) + EPS) * gain)
    return jnp.concatenate(ys, axis=1)


def _merge_kernel(final, ogf_ref, ogb_ref, gg_ref, ohf_ref, ohb_ref, hg_ref, ya_ref, mg_ref, x_ref,
                  mod_ref, gn_gla_ref, gn_hg_ref, wbr_ref, wout_ref, *rest):
    for r in range(x_ref.shape[0] // MERGE_ROWS):
        rows = slice(r * MERGE_ROWS, (r + 1) * MERGE_ROWS)
        o_gla = ogf_ref[rows, :].astype(F32) + ogb_ref[rows, :].astype(F32)
        o_hg = ohf_ref[rows, :].astype(F32) + ohb_ref[rows, :].astype(F32)
        y_gla = _head_norm(o_gla, gn_gla_ref[...], GLA_HEADS, GLA_HV) * _silu(gg_ref[rows, :].astype(F32))
        y_hg = _head_norm(o_hg, gn_hg_ref[...], HGRN_HEADS, HGRN_HV) * _silu(hg_ref[rows, :].astype(F32))
        ys = (y_gla.astype(BF16), y_hg.astype(BF16), ya_ref[rows, :])
        merged = None
        for n in range(N_BRANCH):
            proj = jnp.dot(ys[n], wbr_ref[n], preferred_element_type=F32)
            term = jax.nn.sigmoid(mg_ref[rows, n * D_MODEL:(n + 1) * D_MODEL].astype(F32)) * proj
            merged = term if merged is None else merged + term
        upd = jnp.dot(merged.astype(BF16), wout_ref[...], preferred_element_type=F32)
        x_new = x_ref[rows, :] + mod_ref[0][:, 2 * D_MODEL:] * upd
        if final:
            fg_ref, out_ref = rest
            out_ref[rows, :] = (x_new * lax.rsqrt(jnp.mean(x_new * x_new, axis=-1, keepdims=True) + EPS)
                                * fg_ref[...])
        else:
            ng_ref, nmod_ref, xo_ref, h_ref = rest
            xo_ref[rows, :] = x_new
            h_ref[rows, :] = _norm_mod(x_new, ng_ref[...], nmod_ref[0])


def _merge(final, context, og, oh, y_att, p16, x2d, mod_l, gn_gla, gn_hg, wbr, wout, tail):
    tm = CTX_TILE if context else LAT_TILE
    rows = x2d.shape[0]
    blk0 = R_LAT // tm if context else 0
    mod_row = _ctx_mod_row if context else _lat_mod_row

    def shared(width, col=0):
        return pl.BlockSpec((tm, width), lambda t: (blk0 + t, col))

    def whole(shape):
        return pl.BlockSpec(shape, lambda t: (0,) * len(shape), pipeline_mode=pl.Buffered(1))

    ins = [
        shared(1024), shared(1024), shared(1024, _P16_OFF["ga_g"] // 1024),
        shared(1024), shared(1024), shared(1024, _P16_OFF["hg_g"] // 1024),
        shared(1024), shared(3072, _P16_OFF["mg"] // 3072),
        pl.BlockSpec((tm, D_MODEL), lambda t: (t, 0)),
        pl.BlockSpec((1, 1, 3 * D_MODEL), lambda t: (mod_row(t), 0, 0)),
        whole((1, GLA_HV)), whole((1, HGRN_HV)),
        whole((N_BRANCH, D_MODEL, D_MODEL)), whole((D_MODEL, D_MODEL)),
        whole((1, D_MODEL)),
    ]
    args = [og[0], og[1], p16, oh[0], oh[1], p16, y_att, p16, x2d, mod_l,
            gn_gla.reshape(1, GLA_HV), gn_hg.reshape(1, HGRN_HV), wbr, wout, tail[0].reshape(1, D_MODEL)]
    row_spec = pl.BlockSpec((tm, D_MODEL), lambda t: (t, 0))
    if final:
        out_specs = row_spec
        out_shape = jax.ShapeDtypeStruct((rows, D_MODEL), F32)
    else:
        ins.append(pl.BlockSpec((1, 1, 3 * D_MODEL), lambda t: (mod_row(t), 0, 0)))
        args.append(tail[1])
        out_specs = [row_spec, row_spec]
        out_shape = [jax.ShapeDtypeStruct((rows, D_MODEL), F32), jax.ShapeDtypeStruct((rows, D_MODEL), BF16)]
    return pl.pallas_call(
        functools.partial(_merge_kernel, final),
        grid=(rows // tm,),
        in_specs=ins,
        out_specs=out_specs,
        out_shape=out_shape,
        compiler_params=pltpu.CompilerParams(vmem_limit_bytes=VMEM_LIMIT),
        name="merge_final" if final else ("merge_ctx" if context else "merge"),
    )(*args)


def _rope_table():
    r = ATT_HD // 4
    inv = ROPE_BASE ** (-jnp.arange(r, dtype=F32) / r)
    t = jnp.arange(SEQ)
    ang_row = (t // GRID_W).astype(F32)[:, None] * inv
    ang_col = (t % GRID_W).astype(F32)[:, None] * inv
    cos = jnp.concatenate([jnp.cos(ang_row)] * 2 + [jnp.cos(ang_col)] * 2, axis=1)
    sin = jnp.concatenate([-jnp.sin(ang_row), jnp.sin(ang_row), -jnp.sin(ang_col), jnp.sin(ang_col)], axis=1)
    tab = jnp.concatenate([cos, sin], axis=1)
    ident = jnp.concatenate([jnp.ones((CTX_LEN, ATT_HD), F32), jnp.zeros((CTX_LEN, ATT_HD), F32)], axis=1)
    return jnp.concatenate([tab, ident], axis=0)


def _cols(w, name):
    o, s = _IN_OFF[name]
    return w[:, o:o + s] * _Q_SCALE.get(name, 1.0)


def kernel(x, c, ctx, c_ctx, norm_g, w_ada, b_ada, w_in, gla_w_a2, gla_b_a2, gla_norm_g, hgrn_lb_logits,
           hgrn_norm_g, attn_sink, w_branch, w_out, final_g):
    cc = jnp.zeros((16, D_MODEL), F32).at[:BATCH].set(c).at[BATCH].set(c_ctx)
    mod = _modulation(cc, w_ada, b_ada).reshape(DEPTH, 16, 1, 3 * D_MODEL)
    rope_tab = _rope_table()
    lb_cum = jnp.cumsum(jax.nn.softmax(hgrn_lb_logits.astype(F32), axis=0), axis=0)
    lower_bounds = lb_cum - lb_cum[0]

    xl = x.reshape(R_LAT, D_MODEL)
    xc = ctx.reshape(R_CTX, D_MODEL)
    hl = _norm_modulate(xl, norm_g[0], mod[0], LAT_TILE, _lat_mod_row)
    hc = _norm_modulate(xc, norm_g[0], mod[0], CTX_TILE, _ctx_mod_row)
    out = None
    for l in range(DEPTH):
        final = l == DEPTH - 1
        w = w_in[l]
        w16 = jnp.concatenate([_cols(w, n) for n in _P16_ORDER], axis=1).astype(BF16)
        w_f = _cols(w, "hg_f").astype(BF16)
        w_lr = jnp.concatenate([_cols(w, "ga_lr"), jnp.zeros((D_MODEL, LR_PAD - 2 * GLA_RANK), F32)],
                               axis=1).astype(BF16)
        wz = jnp.zeros((LR_PAD, 2 * GLA_HEADS * GLA_HK), F32)
        wz = wz.at[:GLA_RANK, :GLA_HEADS * GLA_HK].set(gla_w_a2[l, 0])
        wz = wz.at[GLA_RANK:2 * GLA_RANK, GLA_HEADS * GLA_HK:].set(gla_w_a2[l, 1]).astype(BF16)
        bz = gla_b_a2[l].reshape(1, 2 * GLA_HEADS * GLA_HK)
        lb_row = lower_bounds[l].reshape(1, 2 * HGRN_F)

        p16 = _in_proj(hl, hc, w16, BF16, 512)
        p_lr = _in_proj(hl, hc, w_lr, F32, LR_PAD)
        log_f = _forget_proj(hl, hc, w_f, lb_row, l == 0)
        og = _gla(p16, p_lr, wz, bz)
        oh = _hgrn(p16, log_f)
        y_att = _attention(p16, attn_sink[l], rope_tab)
        common = (og, oh, y_att, p16)
        params = (mod[l], gla_norm_g[l], hgrn_norm_g[l], w_branch[l].astype(BF16), w_out[l].astype(BF16))
        if final:
            out = _merge(True, False, *common, xl, *params, (final_g,))
        else:
            tail = (norm_g[l + 1], mod[l + 1])
            xl, hl = _merge(False, False, *common, xl, *params, tail)
            xc, hc = _merge(False, True, *common, xc, *params, tail)
    return out.reshape(BATCH, SEQ, D_MODEL)
```

```python
import functools

import numpy as np
import jax
import jax.numpy as jnp
from jax import lax
from jax.experimental import pallas as pl
from jax.experimental.pallas import tpu as pltpu

F32 = jnp.float32
BF16 = jnp.bfloat16

D_MODEL = 1024
BATCH = 8
SEQ = 4096
DEPTH = 2
CTX_LEN = 256
GRID_W = 64
N_BRANCH = 3
GLA_HEADS = 4
GLA_HV = 256
GLA_HK = 128
GLA_RANK = 16
GLA_TAU = 16.0
HGRN_HEADS = 8
HGRN_HV = 128
HGRN_HK = 128
HGRN_F = HGRN_HEADS * HGRN_HK
ATT_HD = 128
ATT_HQ = 8
ATT_HKV = 2
ATT_GRP = ATT_HQ // ATT_HKV
WINDOW = 128
ATT_BLOCK = 128
ROPE_BASE = 10000.0
EPS = 1e-6

R_LAT = BATCH * SEQ
R_CTX = BATCH * CTX_LEN
ROWS = R_LAT + R_CTX

_IN_SIZES = (512, 512, 1024, 1024, 32, 1024, 2048, 1024, 1024, 1024, 256, 256, 1024, 3072)
_IN_NAMES = ("ga_q", "ga_k", "ga_v", "ga_g", "ga_lr", "hg_q", "hg_f", "hg_i", "hg_g",
             "wa_q", "wa_k", "wa_v", "wa_g", "mg")
_IN_OFF = {}
_o = 0
for _n, _s in zip(_IN_NAMES, _IN_SIZES):
    _IN_OFF[_n] = (_o, _s)
    _o += _s

_P16_ORDER = ("mg", "ga_v", "ga_g", "hg_q", "hg_i", "hg_g", "wa_q", "wa_g", "ga_q", "ga_k", "wa_k", "wa_v")
_P16_OFF = {}
_o = 0
for _n in _P16_ORDER:
    _P16_OFF[_n] = _o
    _o += _IN_OFF[_n][1]
P16_COLS = _o
LR_PAD = 256
LOG2E = 1.4426950408889634
_Q_SCALE = {"ga_q": GLA_HK ** -0.5, "wa_q": ATT_HD ** -0.5 * LOG2E}

VMEM_LIMIT = 56 * 1024 * 1024


def _nt(a, b):
    return lax.dot_general(a, b, (((1,), (1,)), ((), ())), preferred_element_type=F32)


def _tn(a, b):
    return lax.dot_general(a, b, (((0,), (0,)), ((), ())), preferred_element_type=F32)


def _log_sigmoid(z):
    return jnp.minimum(z, 0.0) - jnp.log(1.0 + jnp.exp(-jnp.abs(z)))


def _silu(x):
    return x * jax.nn.sigmoid(x)


def _norm_mod(x, g, m):
    y = x * lax.rsqrt(jnp.mean(x * x, axis=-1, keepdims=True) + EPS) * g
    return (y * (1.0 + m[:, D_MODEL:2 * D_MODEL]) + m[:, :D_MODEL]).astype(BF16)


def _mod_kernel(c_ref, w_ref, b_ref, o_ref):
    a = _silu(c_ref[...])
    o_ref[0] = jnp.dot(a.astype(BF16), w_ref[0].astype(BF16), preferred_element_type=F32) + b_ref[0]


def _modulation(cc, w_ada, b_ada):
    tn = 512
    return pl.pallas_call(
        _mod_kernel,
        grid=(DEPTH, 3 * D_MODEL // tn),
        in_specs=[
            pl.BlockSpec((16, D_MODEL), lambda l, j: (0, 0)),
            pl.BlockSpec((1, D_MODEL, tn), lambda l, j: (l, 0, j)),
            pl.BlockSpec((1, 1, tn), lambda l, j: (l, 0, j)),
        ],
        out_specs=pl.BlockSpec((1, 16, tn), lambda l, j: (l, 0, j)),
        out_shape=jax.ShapeDtypeStruct((DEPTH, 16, 3 * D_MODEL), F32),
        name="modulation",
    )(cc, w_ada, b_ada.reshape(DEPTH, 1, 3 * D_MODEL))


LAT_TILE = 512
CTX_TILE = 256
MERGE_ROWS = 256


def _lat_mod_row(t):
    return t // (SEQ // LAT_TILE)


def _ctx_mod_row(t):
    return BATCH


def _norm_kernel(x_ref, g_ref, mod_ref, h_ref):
    h_ref[...] = _norm_mod(x_ref[...], g_ref[...], mod_ref[0])


def _norm_modulate(x2d, norm_g, mod_l, tm, mod_row):
    rows = x2d.shape[0]
    return pl.pallas_call(
        _norm_kernel,
        grid=(rows // tm,),
        in_specs=[
            pl.BlockSpec((tm, D_MODEL), lambda t: (t, 0)),
            pl.BlockSpec((1, D_MODEL), lambda t: (0, 0)),
            pl.BlockSpec((1, 1, 3 * D_MODEL), lambda t: (mod_row(t), 0, 0)),
        ],
        out_specs=pl.BlockSpec((tm, D_MODEL), lambda t: (t, 0)),
        out_shape=jax.ShapeDtypeStruct((rows, D_MODEL), BF16),
        name="norm_modulate",
    )(x2d, norm_g.reshape(1, D_MODEL), mod_l)


PROJ_TILE = 2048
N_LAT_PROJ = R_LAT // PROJ_TILE
N_CTX_PROJ = R_CTX // PROJ_TILE


def _mm_kernel(hl_ref, hc_ref, w_ref, o_ref):
    i = pl.program_id(0)

    @pl.when(i < N_LAT_PROJ)
    def _():
        o_ref[...] = jnp.dot(hl_ref[...], w_ref[...], preferred_element_type=F32).astype(o_ref.dtype)

    @pl.when(i >= N_LAT_PROJ)
    def _():
        o_ref[...] = jnp.dot(hc_ref[...], w_ref[...], preferred_element_type=F32).astype(o_ref.dtype)


def _proj_specs(k, tn):
    tm = PROJ_TILE
    return [
        pl.BlockSpec((tm, k), lambda i, j: (jnp.minimum(i, N_LAT_PROJ - 1), 0)),
        pl.BlockSpec((tm, k), lambda i, j: (jnp.maximum(i - N_LAT_PROJ, 0), 0)),
        pl.BlockSpec((k, tn), lambda i, j: (0, j)),
    ]


def _in_proj(hl, hc, w, out_dtype, tn):
    k, n = w.shape
    return pl.pallas_call(
        _mm_kernel,
        grid=(N_LAT_PROJ + N_CTX_PROJ, n // tn),
        in_specs=_proj_specs(k, tn),
        out_specs=pl.BlockSpec((PROJ_TILE, tn), lambda i, j: (i, j)),
        out_shape=jax.ShapeDtypeStruct((ROWS, n), out_dtype),
        compiler_params=pltpu.CompilerParams(vmem_limit_bytes=VMEM_LIMIT),
        name="in_proj",
    )(hl, hc, w)


GATE_ROWS = 256


def _forget_proj_kernel(zero_bound, hl_ref, hc_ref, w_ref, lb_ref, o_ref):
    i = pl.program_id(0)
    lb = lb_ref[...]

    def run(h_ref):
        for r in range(PROJ_TILE // GATE_ROWS):
            rows = slice(r * GATE_ROWS, (r + 1) * GATE_ROWS)
            z = jnp.dot(h_ref[rows, :], w_ref[...], preferred_element_type=F32)
            t = jnp.exp(-jnp.abs(z))
            log_num = jnp.minimum(z, 0.0)
            if not zero_bound:
                log_num = jnp.maximum(jnp.log(jnp.where(z >= 0.0, 1.0 + lb * t, lb + t)), log_num)
            o_ref[rows, :] = log_num - jnp.log(1.0 + t)

    @pl.when(i < N_LAT_PROJ)
    def _():
        run(hl_ref)

    @pl.when(i >= N_LAT_PROJ)
    def _():
        run(hc_ref)


def _forget_proj(hl, hc, w, lb_row, zero_bound):
    k, n = w.shape
    tn = 256
    return pl.pallas_call(
        functools.partial(_forget_proj_kernel, zero_bound),
        grid=(N_LAT_PROJ + N_CTX_PROJ, n // tn),
        in_specs=_proj_specs(k, tn) + [pl.BlockSpec((1, tn), lambda i, j: (0, j))],
        out_specs=pl.BlockSpec((PROJ_TILE, tn), lambda i, j: (i, j)),
        out_shape=jax.ShapeDtypeStruct((ROWS, n), F32),
        compiler_params=pltpu.CompilerParams(vmem_limit_bytes=VMEM_LIMIT),
        name="forget_proj",
    )(hl, hc, w, lb_row)


CHUNK = 128
SUB = 16
N_SUB = CHUNK // SUB
STEP_CHUNKS = 2
STEP_ROWS = STEP_CHUNKS * CHUNK
LAT_CHUNKS = SEQ // STEP_ROWS
CTX_CHUNKS = CTX_LEN // STEP_ROWS
N_CHUNK = LAT_CHUNKS + CTX_CHUNKS
CTX_CHUNK0 = R_LAT // STEP_ROWS

_PAIR_LEVELS = tuple(SUB * 2 ** i for i in range(N_SUB.bit_length() - 1))
_GROUPS = tuple(2 ** (i + 1) for i in range(N_SUB.bit_length() - 1))
_FAC_ROWS = -(-8 * (2 * len(_GROUPS) + 1) // 16) * 16


def _fwd_rows(b, n):
    return jnp.where(n < CTX_CHUNKS, CTX_CHUNK0 + CTX_CHUNKS * b + n, LAT_CHUNKS * b + n - CTX_CHUNKS)


def _bwd_rows(b, n):
    return jnp.where(n < CTX_CHUNKS, CTX_CHUNK0 + CTX_CHUNKS * b + CTX_CHUNKS - 1 - n,
                     LAT_CHUNKS * b + N_CHUNK - 1 - n)


def _level_matrix(reverse):
    nb = N_SUB
    pos = [nb - 1 - m for m in range(nb)] if reverse else list(range(nb))
    mat = np.zeros((_FAC_ROWS, 16), np.float32)
    for gi, r in enumerate(_GROUPS):
        for m in range(nb):
            for m2 in range(nb):
                if m2 // r == m // r:
                    if pos[m2] < pos[m]:
                        mat[8 * gi + m, m2] = 1.0
                    if pos[m2] > pos[m]:
                        mat[8 * (len(_GROUPS) + gi) + m, m2] = 1.0
    mat[8 * 2 * len(_GROUPS), :nb] = 1.0
    return mat


def _cumsum_matrix(reverse):
    i = np.arange(CHUNK)[:, None]
    j = np.arange(CHUNK)[None, :]
    t = ((i // SUB == j // SUB) & ((j >= i) if reverse else (j <= i))).astype(np.float32)
    return np.concatenate([t, t], axis=1)


def _mxu_cumsum(la, cmat):
    hi = la.astype(BF16)
    lo = (la - hi.astype(F32)).astype(BF16)
    return jnp.dot(cmat, jnp.concatenate([hi, lo], axis=0), preferred_element_type=F32)


def _rep_rows(x):
    return jnp.concatenate([jnp.broadcast_to(x[m:m + 1], (SUB, x.shape[1])) for m in range(N_SUB)], axis=0)


def _split3(x):
    hi = x.astype(BF16)
    r1 = x - hi.astype(F32)
    mid = r1.astype(BF16)
    lo = (r1 - mid.astype(F32)).astype(BF16)
    return hi, mid, lo


def _block_factors(bl, lmat, reverse):
    last = [SUB * m + (0 if reverse else SUB - 1) for m in range(N_SUB)]
    tot = jnp.concatenate([bl[i:i + 1] for i in last], axis=0)
    tot16 = jnp.concatenate([tot, jnp.zeros((16 - N_SUB, tot.shape[1]), F32)], axis=0)
    logs = sum(jnp.dot(lmat, t, preferred_element_type=F32) for t in _split3(tot16))
    return tot, jnp.exp(logs)


def _decay_variants(q, k, bl, tot, fac):
    qe = q * jnp.exp(bl)
    ke = k * jnp.exp(_rep_rows(tot) - bl)
    q_lv = {SUB: qe.astype(BF16)}
    k_lv = {SUB: ke.astype(BF16), "diag": (k * jnp.exp(-bl)).astype(BF16)}
    for gi in range(len(_GROUPS)):
        s = 2 * SUB * 2 ** gi
        q_lv[s] = (qe * _rep_rows(fac[8 * gi:8 * gi + N_SUB])).astype(BF16)
        fo = 8 * (len(_GROUPS) + gi)
        k_lv[s] = (ke * _rep_rows(fac[fo:fo + N_SUB])).astype(BF16)
    decay = fac[8 * 2 * len(_GROUPS):8 * 2 * len(_GROUPS) + 1]
    return q_lv, k_lv, decay


def _lead_rows(x, s, reverse):
    first = 0 if reverse else 1
    pieces = [x[(2 * g + first) * s:(2 * g + first + 1) * s] for g in range(CHUNK // (2 * s))]
    return pieces[0] if len(pieces) == 1 else jnp.concatenate(pieces, axis=0)


def _scan_masks(reverse):
    n = CHUNK
    i = lax.broadcasted_iota(jnp.int32, (n, n), 0)
    j = lax.broadcasted_iota(jnp.int32, (n, n), 1)
    sh = SUB.bit_length() - 1
    same = jnp.right_shift(i, sh) == jnp.right_shift(j, sh)
    masks = {"diag": jnp.logical_and(same, j >= i if reverse else j <= i)}
    rc = lax.broadcasted_iota(jnp.int32, (n // 2, n), 0)
    jc = lax.broadcasted_iota(jnp.int32, (n // 2, n), 1)
    for s in _PAIR_LEVELS:
        sh = s.bit_length() - 1
        masks[s] = jnp.right_shift(jc, sh) == 2 * jnp.right_shift(rc, sh) + (1 if reverse else 0)
    return masks


def _intra_scores(q_lv, k_lv, cols, masks, reverse):
    diag = jnp.where(masks["diag"], _nt(q_lv[SUB][:, cols], k_lv["diag"][:, cols]), 0.0)
    lead = {}
    for s in _PAIR_LEVELS:
        lead[s] = jnp.where(masks[s], _nt(_lead_rows(q_lv[s][:, cols], s, reverse), k_lv[s][:, cols]), 0.0)
    blocks = []
    for m in range(N_SUB):
        blk = diag[SUB * m:SUB * (m + 1)]
        for s in _PAIR_LEVELS:
            r = s // SUB
            ms = m // r
            if ms % 2 == (0 if reverse else 1):
                off = (ms // 2) * s + (m % r) * SUB
                blk = blk + lead[s][off:off + SUB]
        blocks.append(blk)
    return jnp.concatenate(blocks, axis=0).astype(BF16)


def _step_chunks(reverse):
    order = range(STEP_CHUNKS - 1, -1, -1) if reverse else range(STEP_CHUNKS)
    return [slice(c * CHUNK, (c + 1) * CHUNK) for c in order]


def _scan_step(directions, heads, hk, hv):
    chunks = [(d, rows) for d in directions for rows in _step_chunks(d[6])]
    bls = [d[0](rows) for d, rows in chunks]
    facs = [_block_factors(bl, d[2], d[6]) for (d, _), bl in zip(chunks, bls)]
    variants = [_decay_variants(*d[1](rows), bl, tot, fac) for (d, rows), bl, (tot, fac) in zip(chunks, bls, facs)]
    masks = {d[6]: _scan_masks(d[6]) for d in directions}
    heads_of = [(slice(h * hk, (h + 1) * hk), slice(h * hv, (h + 1) * hv)) for h in range(heads)]
    atts = [[_intra_scores(q_lv, k_lv, ks, masks[d[6]], d[6]) for ks, _ in heads_of]
            for (d, _), (q_lv, k_lv, _) in zip(chunks, variants)]
    intra = [[jnp.dot(a, d[3][rows, vs], preferred_element_type=F32) for a, (_, vs) in zip(att, heads_of)]
             for (d, rows), att in zip(chunks, atts)]
    for (d, rows), (q_lv, k_lv, decay), o_in in zip(chunks, variants, intra):
        _, _, _, v_ref, o_ref, st_ref, _ = d
        for h, (ks, vs) in enumerate(heads_of):
            st = st_ref[h]
            o_ref[rows, vs] = (o_in[h] + _nt(q_lv[CHUNK][:, ks], st.astype(BF16))).astype(o_ref.dtype)
            st_ref[h] = decay[:, ks] * st + _tn(v_ref[rows, vs], k_lv[CHUNK][:, ks])


def _reset_state(stf_ref, stb_ref):
    @pl.when(pl.program_id(1) == 0)
    def _():
        stf_ref[...] = jnp.zeros_like(stf_ref)
        stb_ref[...] = jnp.zeros_like(stb_ref)


def _gla_kernel(qf_ref, kf_ref, vf_ref, lrf_ref, qb_ref, kb_ref, vb_ref, lrb_ref, wz_ref, bz_ref,
                cmat_ref, lmat_ref, of_ref, ob_ref, stf_ref, stb_ref):
    _reset_state(stf_ref, stb_ref)
    width = GLA_HEADS * GLA_HK
    gates = [jnp.dot(lr_ref[...].astype(BF16), wz_ref[:, d * width:(d + 1) * width], preferred_element_type=F32)
             + bz_ref[:, d * width:(d + 1) * width] for d, lr_ref in enumerate((lrf_ref, lrb_ref))]
    directions = []
    for d, (q_ref, k_ref, v_ref, o_ref, st_ref) in enumerate((
            (qf_ref, kf_ref, vf_ref, of_ref, stf_ref),
            (qb_ref, kb_ref, vb_ref, ob_ref, stb_ref))):
        reverse = d == 1

        def bl_of(rows, d=d):
            return _mxu_cumsum(_log_sigmoid(gates[d][rows]) / GLA_TAU, cmat_ref[d])

        def qk_of(rows, q_ref=q_ref, k_ref=k_ref):
            return q_ref[rows, :].astype(F32), k_ref[rows, :].astype(F32)

        directions.append((bl_of, qk_of, lmat_ref[d], v_ref, o_ref, st_ref, reverse))
    _scan_step(directions, GLA_HEADS, GLA_HK, GLA_HV)


def _scan_call(kernel_fn, ins, args, name, heads, hk, hv):
    out_sds = jax.ShapeDtypeStruct((ROWS, heads * hv), BF16)
    consts = [np.stack([_cumsum_matrix(False), _cumsum_matrix(True)]),
              np.stack([_level_matrix(False), _level_matrix(True)])]
    return pl.pallas_call(
        kernel_fn,
        grid=(BATCH, N_CHUNK),
        in_specs=ins + [pl.BlockSpec(c.shape, lambda b, n: (0, 0, 0)) for c in consts],
        out_specs=[pl.BlockSpec((STEP_ROWS, heads * hv), lambda b, n: (_fwd_rows(b, n), 0)),
                   pl.BlockSpec((STEP_ROWS, heads * hv), lambda b, n: (_bwd_rows(b, n), 0))],
        out_shape=[out_sds, out_sds],
        scratch_shapes=[pltpu.VMEM((heads, hv, hk), F32), pltpu.VMEM((heads, hv, hk), F32)],
        compiler_params=pltpu.CompilerParams(dimension_semantics=("arbitrary", "arbitrary"),
                                             vmem_limit_bytes=VMEM_LIMIT),
        name=name,
    )(*args, *[jnp.asarray(c, dtype=BF16) for c in consts])


def _chunk_spec(width, blk, rows_of):
    return pl.BlockSpec((STEP_ROWS, width), lambda b, n: (rows_of(b, n), blk))


def _gla(p16, p_lr, wz, bz):
    q_blk = _P16_OFF["ga_q"] // 512
    k_blk = _P16_OFF["ga_k"] // 512
    v_blk = _P16_OFF["ga_v"] // 1024
    ins, args = [], []
    for rows_of in (_fwd_rows, _bwd_rows):
        ins += [_chunk_spec(512, q_blk, rows_of), _chunk_spec(512, k_blk, rows_of),
                _chunk_spec(1024, v_blk, rows_of), _chunk_spec(LR_PAD, 0, rows_of)]
        args += [p16, p16, p16, p_lr]
    ins += [pl.BlockSpec((LR_PAD, 1024), lambda b, n: (0, 0)), pl.BlockSpec((1, 1024), lambda b, n: (0, 0))]
    args += [wz, bz]
    return _scan_call(_gla_kernel, ins, args, "gla_scan", GLA_HEADS, GLA_HK, GLA_HV)


def _hgrn_kernel(qf_ref, if_ref, ff_ref, qb_ref, ib_ref, fb_ref, cmat_ref, lmat_ref, of_ref, ob_ref,
                 stf_ref, stb_ref):
    _reset_state(stf_ref, stb_ref)
    directions = []
    for d, (q_ref, i_ref, f_ref, o_ref, st_ref) in enumerate((
            (qf_ref, if_ref, ff_ref, of_ref, stf_ref),
            (qb_ref, ib_ref, fb_ref, ob_ref, stb_ref))):

        def bl_of(rows, d=d, f_ref=f_ref):
            return _mxu_cumsum(f_ref[rows, :], cmat_ref[d])

        def qk_of(rows, q_ref=q_ref, f_ref=f_ref):
            return q_ref[rows, :].astype(F32), 1.0 - jnp.exp(f_ref[rows, :])

        directions.append((bl_of, qk_of, lmat_ref[d], i_ref, o_ref, st_ref, d == 1))
    _scan_step(directions, HGRN_HEADS, HGRN_HK, HGRN_HV)


def _hgrn(p16, log_f):
    q_blk = _P16_OFF["hg_q"] // 1024
    i_blk = _P16_OFF["hg_i"] // 1024
    ins, args = [], []
    for d, rows_of in enumerate((_fwd_rows, _bwd_rows)):
        ins += [_chunk_spec(1024, blk, rows_of) for blk in (q_blk, i_blk, d)]
        args += [p16, p16, log_f]
    return _scan_call(_hgrn_kernel, ins, args, "hgrn_scan", HGRN_HEADS, HGRN_HK, HGRN_HV)


Q_STEP = 2 * ATT_BLOCK
LAT_BLKS = SEQ // ATT_BLOCK
LAT_STEPS = SEQ // Q_STEP
CTX_STEPS = CTX_LEN // Q_STEP
_FAR = 4 * ATT_BLOCK


def _rope(x, tab):
    lane = lax.broadcasted_iota(jnp.int32, x.shape, 1)
    partner = jnp.where(jnp.bitwise_and(lane, 32) == 0, pltpu.roll(x, 96, 1), pltpu.roll(x, 32, 1))
    return x * tab[:, :ATT_HD] + partner * tab[:, ATT_HD:]


def _window_mask(jblk, is_lat):
    blk = ATT_BLOCK
    off_prev = jnp.where(jnp.logical_and(is_lat, jblk >= 1), 0, _FAR)
    off_cur = jnp.where(is_lat, 0, _FAR)
    off_next = jnp.where(jnp.logical_and(is_lat, jblk + 1 < LAT_BLKS), 0, _FAR)
    qi = lax.broadcasted_iota(jnp.int32, (blk, 3 * blk), 0)
    col = lax.broadcasted_iota(jnp.int32, (blk, 3 * blk), 1)
    land, lor = jnp.logical_and, jnp.logical_or
    in_prev = land(col < blk, col >= qi + off_prev)
    in_cur = land(col >= blk + off_cur, col < 2 * blk)
    in_next = land(col >= 2 * blk, col - 2 * blk + off_next <= qi)
    return lor(in_prev, lor(in_cur, in_next))


def _attn_kernel(sink_ref, q_ref, g_ref, kvc_ref, kvm_ref, kvp_ref, kvn_ref, tm_ref, tp_ref, tn_ref, o_ref):
    step = pl.program_id(1)
    is_lat = step < LAT_STEPS
    blk = ATT_BLOCK
    kvw = ATT_HKV * ATT_HD
    tmid = tm_ref[...]
    tabs = [tp_ref[...], tmid[:blk], tmid[blk:], tn_ref[...]]
    masks = [_window_mask(2 * step + sub, is_lat) for sub in range(2)]
    units = []
    for kvh in range(ATT_HKV):
        kc = slice(kvh * ATT_HD, (kvh + 1) * ATT_HD)
        vc = slice(kvw + kvh * ATT_HD, kvw + (kvh + 1) * ATT_HD)
        heads = [kvh * ATT_GRP + g for g in range(ATT_GRP)]
        k_src = [kvp_ref[:, kc], kvm_ref[:blk, kc], kvm_ref[blk:, kc], kvn_ref[:, kc]]
        k_rot = [_rope(k.astype(F32), t).astype(BF16) for k, t in zip(k_src, tabs)]
        v_src = [kvp_ref[:, vc], kvm_ref[:blk, vc], kvm_ref[blk:, vc], kvn_ref[:, vc]]
        for sub in range(2):
            rows = slice(sub * blk, (sub + 1) * blk)
            qs = jnp.concatenate([_rope(q_ref[rows, h * ATT_HD:(h + 1) * ATT_HD].astype(F32), tabs[1 + sub])
                                  for h in heads], axis=0).astype(BF16)
            k_all = jnp.concatenate([kvc_ref[:, kc]] + k_rot[sub:sub + 3], axis=0)
            v_all = jnp.concatenate([kvc_ref[:, vc]] + v_src[sub:sub + 3], axis=0)
            units.append((heads, sub, rows, qs, k_all, v_all))
    scores = [_nt(qs, k_all) for _, _, _, qs, k_all, _ in units]
    probs = []
    for (heads, sub, _, _, _, _), s in zip(units, scores):
        ps, ls = [], []
        for g, h in enumerate(heads):
            s_ctx = s[g * blk:(g + 1) * blk, :CTX_LEN]
            s_loc = jnp.where(masks[sub], s[g * blk:(g + 1) * blk, CTX_LEN:], -jnp.inf)
            sk = sink_ref[h] * LOG2E
            m = jnp.maximum(jnp.maximum(jnp.max(s_ctx, axis=-1, keepdims=True),
                                        jnp.max(s_loc, axis=-1, keepdims=True)), sk)
            p = jnp.concatenate([jnp.exp2(s_ctx - m), jnp.exp2(s_loc - m)], axis=1)
            ls.append(jnp.sum(p, axis=-1, keepdims=True) + jnp.exp2(sk - m))
            ps.append(p.astype(BF16))
        probs.append((jnp.concatenate(ps, axis=0), ls))
    outs = [jnp.dot(p, v_all, preferred_element_type=F32) for (p, _), (_, _, _, _, _, v_all) in zip(probs, units)]
    for (heads, _, rows, _, _, _), (_, ls), out in zip(units, probs, outs):
        for g, h in enumerate(heads):
            hs = slice(h * ATT_HD, (h + 1) * ATT_HD)
            gate = _silu(g_ref[rows, hs].astype(F32))
            o_ref[rows, hs] = (out[g * blk:(g + 1) * blk, :] * (1.0 / ls[g]) * gate).astype(o_ref.dtype)


def _attention(p16, sink, rope_tab):
    blk = ATT_BLOCK
    qw = ATT_HQ * ATT_HD
    kvw = 2 * ATT_HKV * ATT_HD
    q_blk = _P16_OFF["wa_q"] // qw
    g_blk = _P16_OFF["wa_g"] // qw
    kv_blk = _P16_OFF["wa_k"] // kvw
    lat_q = R_LAT // Q_STEP

    def q_rows(b, s):
        return jnp.where(s < LAT_STEPS, LAT_STEPS * b + s, lat_q + CTX_STEPS * b + s - LAT_STEPS)

    def prev_blk(s):
        return jnp.clip(2 * s - 1, 0, LAT_BLKS - 1)

    def next_blk(s):
        return jnp.clip(2 * s + 2, 0, LAT_BLKS - 1)

    ins = [
        pl.BlockSpec(memory_space=pltpu.SMEM),
        pl.BlockSpec((Q_STEP, qw), lambda b, s: (q_rows(b, s), q_blk)),
        pl.BlockSpec((Q_STEP, qw), lambda b, s: (q_rows(b, s), g_blk)),
        pl.BlockSpec((CTX_LEN, kvw), lambda b, s: (R_LAT // CTX_LEN + b, kv_blk)),
        pl.BlockSpec((Q_STEP, kvw), lambda b, s: (q_rows(b, s), kv_blk)),
        pl.BlockSpec((blk, kvw), lambda b, s: (LAT_BLKS * b + prev_blk(s), kv_blk)),
        pl.BlockSpec((blk, kvw), lambda b, s: (LAT_BLKS * b + next_blk(s), kv_blk)),
        pl.BlockSpec((Q_STEP, 2 * ATT_HD), lambda b, s: (s, 0)),
        pl.BlockSpec((blk, 2 * ATT_HD), lambda b, s: (prev_blk(s), 0)),
        pl.BlockSpec((blk, 2 * ATT_HD), lambda b, s: (next_blk(s), 0)),
    ]
    return pl.pallas_call(
        _attn_kernel,
        grid=(BATCH, LAT_STEPS + CTX_STEPS),
        in_specs=ins,
        out_specs=pl.BlockSpec((Q_STEP, qw), lambda b, s: (q_rows(b, s), 0)),
        out_shape=jax.ShapeDtypeStruct((ROWS, qw), BF16),
        compiler_params=pltpu.CompilerParams(vmem_limit_bytes=VMEM_LIMIT),
        name="window_attn",
    )(sink, p16, p16, p16, p16, p16, p16, rope_tab, rope_tab, rope_tab)


def _head_norm(o, heads, width):
    ys = []
    for h in range(heads):
        oh = o[:, h * width:(h + 1) * width]
        ys.append(oh * lax.rsqrt(jnp.mean(oh * oh, axis=-1, keepdims=True) + EPS))
    return jnp.concatenate(ys, axis=1)


def _merge_kernel(final, ogf_ref, ogb_ref, gg_ref, ohf_ref, ohb_ref, hg_ref, ya_ref, mg_ref, x_ref,
                  mod_ref, wbr_ref, wout_ref, *rest):
    for r in range(x_ref.shape[0] // MERGE_ROWS):
        rows = slice(r * MERGE_ROWS, (r + 1) * MERGE_ROWS)
        o_gla = ogf_ref[rows, :].astype(F32) + ogb_ref[rows, :].astype(F32)
        o_hg = ohf_ref[rows, :].astype(F32) + ohb_ref[rows, :].astype(F32)
        y_gla = _head_norm(o_gla, GLA_HEADS, GLA_HV) * _silu(gg_ref[rows, :].astype(F32))
        y_hg = _head_norm(o_hg, HGRN_HEADS, HGRN_HV) * _silu(hg_ref[rows, :].astype(F32))
        ys = (y_gla.astype(BF16), y_hg.astype(BF16), ya_ref[rows, :])
        merged = None
        for n in range(N_BRANCH):
            proj = jnp.dot(ys[n], wbr_ref[n], preferred_element_type=F32)
            term = jax.nn.sigmoid(mg_ref[rows, n * D_MODEL:(n + 1) * D_MODEL].astype(F32)) * proj
            merged = term if merged is None else merged + term
        upd = jnp.dot(merged.astype(BF16), wout_ref[...], preferred_element_type=F32)
        x_new = x_ref[rows, :] + mod_ref[0][:, 2 * D_MODEL:] * upd
        if final:
            fg_ref, out_ref = rest
            out_ref[rows, :] = (x_new * lax.rsqrt(jnp.mean(x_new * x_new, axis=-1, keepdims=True) + EPS)
                                * fg_ref[...])
        else:
            ng_ref, nmod_ref, xo_ref, h_ref = rest
            xo_ref[rows, :] = x_new
            h_ref[rows, :] = _norm_mod(x_new, ng_ref[...], nmod_ref[0])


def _merge(final, context, og, oh, y_att, p16, x2d, mod_l, wbr, wout, tail):
    tm = CTX_TILE if context else LAT_TILE
    rows = x2d.shape[0]
    blk0 = R_LAT // tm if context else 0
    mod_row = _ctx_mod_row if context else _lat_mod_row

    def shared(width, col=0):
        return pl.BlockSpec((tm, width), lambda t: (blk0 + t, col))

    def whole(shape):
        return pl.BlockSpec(shape, lambda t: (0,) * len(shape), pipeline_mode=pl.Buffered(1))

    ins = [
        shared(1024), shared(1024), shared(1024, _P16_OFF["ga_g"] // 1024),
        shared(1024), shared(1024), shared(1024, _P16_OFF["hg_g"] // 1024),
        shared(1024), shared(3072, _P16_OFF["mg"] // 3072),
        pl.BlockSpec((tm, D_MODEL), lambda t: (t, 0)),
        pl.BlockSpec((1, 1, 3 * D_MODEL), lambda t: (mod_row(t), 0, 0)),
        whole((N_BRANCH, D_MODEL, D_MODEL)), whole((D_MODEL, D_MODEL)),
        whole((1, D_MODEL)),
    ]
    args = [og[0], og[1], p16, oh[0], oh[1], p16, y_att, p16, x2d, mod_l,
            wbr, wout, tail[0].reshape(1, D_MODEL)]
    row_spec = pl.BlockSpec((tm, D_MODEL), lambda t: (t, 0))
    if final:
        out_specs = row_spec
        out_shape = jax.ShapeDtypeStruct((rows, D_MODEL), F32)
    else:
        ins.append(pl.BlockSpec((1, 1, 3 * D_MODEL), lambda t: (mod_row(t), 0, 0)))
        args.append(tail[1])
        out_specs = [row_spec, row_spec]
        out_shape = [jax.ShapeDtypeStruct((rows, D_MODEL), F32), jax.ShapeDtypeStruct((rows, D_MODEL), BF16)]
    return pl.pallas_call(
        functools.partial(_merge_kernel, final),
        grid=(rows // tm,),
        in_specs=ins,
        out_specs=out_specs,
        out_shape=out_shape,
        compiler_params=pltpu.CompilerParams(vmem_limit_bytes=VMEM_LIMIT),
        name="merge_final" if final else ("merge_ctx" if context else "merge"),
    )(*args)


def _rope_table():
    r = ATT_HD // 4
    inv = ROPE_BASE ** (-jnp.arange(r, dtype=F32) / r)
    t = jnp.arange(SEQ)
    ang_row = (t // GRID_W).astype(F32)[:, None] * inv
    ang_col = (t % GRID_W).astype(F32)[:, None] * inv
    cos = jnp.concatenate([jnp.cos(ang_row)] * 2 + [jnp.cos(ang_col)] * 2, axis=1)
    sin = jnp.concatenate([-jnp.sin(ang_row), jnp.sin(ang_row), -jnp.sin(ang_col), jnp.sin(ang_col)], axis=1)
    tab = jnp.concatenate([cos, sin], axis=1)
    ident = jnp.concatenate([jnp.ones((CTX_LEN, ATT_HD), F32), jnp.zeros((CTX_LEN, ATT_HD), F32)], axis=1)
    return jnp.concatenate([tab, ident], axis=0)


def _cols(w, name):
    o, s = _IN_OFF[name]
    return w[:, o:o + s] * _Q_SCALE.get(name, 1.0)


def kernel(x, c, ctx, c_ctx, norm_g, w_ada, b_ada, w_in, gla_w_a2, gla_b_a2, gla_norm_g, hgrn_lb_logits,
           hgrn_norm_g, attn_sink, w_branch, w_out, final_g):
    cc = jnp.zeros((16, D_MODEL), F32).at[:BATCH].set(c).at[BATCH].set(c_ctx)
    mod = _modulation(cc, w_ada, b_ada).reshape(DEPTH, 16, 1, 3 * D_MODEL)
    rope_tab = _rope_table()
    lb_cum = jnp.cumsum(jax.nn.softmax(hgrn_lb_logits.astype(F32), axis=0), axis=0)
    lower_bounds = lb_cum - lb_cum[0]

    xl = x.reshape(R_LAT, D_MODEL)
    xc = ctx.reshape(R_CTX, D_MODEL)
    hl = _norm_modulate(xl, norm_g[0], mod[0], LAT_TILE, _lat_mod_row)
    hc = _norm_modulate(xc, norm_g[0], mod[0], CTX_TILE, _ctx_mod_row)
    out = None
    for l in range(DEPTH):
        final = l == DEPTH - 1
        w = w_in[l]
        w16 = jnp.concatenate([_cols(w, n) for n in _P16_ORDER], axis=1).astype(BF16)
        w_f = _cols(w, "hg_f").astype(BF16)
        w_lr = jnp.concatenate([_cols(w, "ga_lr"), jnp.zeros((D_MODEL, LR_PAD - 2 * GLA_RANK), F32)],
                               axis=1).astype(BF16)
        wz = jnp.zeros((LR_PAD, 2 * GLA_HEADS * GLA_HK), F32)
        wz = wz.at[:GLA_RANK, :GLA_HEADS * GLA_HK].set(gla_w_a2[l, 0])
        wz = wz.at[GLA_RANK:2 * GLA_RANK, GLA_HEADS * GLA_HK:].set(gla_w_a2[l, 1]).astype(BF16)
        bz = gla_b_a2[l].reshape(1, 2 * GLA_HEADS * GLA_HK)
        lb_row = lower_bounds[l].reshape(1, 2 * HGRN_F)

        p16 = _in_proj(hl, hc, w16, BF16, 512)
        p_lr = _in_proj(hl, hc, w_lr, F32, LR_PAD)
        log_f = _forget_proj(hl, hc, w_f, lb_row, l == 0)
        og = _gla(p16, p_lr, wz, bz)
        oh = _hgrn(p16, log_f)
        y_att = _attention(p16, attn_sink[l], rope_tab)
        common = (og, oh, y_att, p16)
        row_gain = jnp.stack([jnp.tile(gla_norm_g[l], GLA_HEADS), jnp.tile(hgrn_norm_g[l], HGRN_HEADS),
                              jnp.ones((D_MODEL,), F32)])
        params = (mod[l], (w_branch[l] * row_gain[:, :, None]).astype(BF16), w_out[l].astype(BF16))
        if final:
            out = _merge(True, False, *common, xl, *params, (final_g,))
        else:
            tail = (norm_g[l + 1], mod[l + 1])
            xl, hl = _merge(False, False, *common, xl, *params, tail)
            xc, hc = _merge(False, True, *common, xc, *params, tail)
    return out.reshape(BATCH, SEQ, D_MODEL)
```
